```python
import jax, jax.numpy as jnp
from jax import lax
import numpy as np

D_MODEL = 2048
BATCH = 2
SEQ = 4096
DEPTH = 2
DEC_BATCH = 8
DEC_SEQ = 32
PAST_LEN = 1024

CHUNK = 64
N_META = 16
HG_HEADS = 16
HG_DK = 128
HG_DV = 128
HG_DIM = HG_HEADS * HG_DK
HG_VDIM = HG_HEADS * HG_DV
SSM_HEADS = 32
SSM_HEAD_DIM = 64
SSM_INNER = SSM_HEADS * SSM_HEAD_DIM
SSM_GROUPS = 4
SSM_HPG = SSM_HEADS // SSM_GROUPS
SSM_STATE = 128
SSM_CONV = 4
SSM_XBC = SSM_INNER + 2 * SSM_GROUPS * SSM_STATE
D_FF = 5632
FFN_CONV = 3
GLA_BLOCK = 32
SSD_BLOCK = 64
ALPHA = (2.0 * DEPTH) ** 0.25
BETA = (8.0 * DEPTH) ** -0.25
LN_EPS = 1e-5
RMS_EPS = 1e-6
SPLIT_SIZES = (HG_DIM, HG_DIM, HG_VDIM, HG_VDIM, SSM_INNER, SSM_XBC, SSM_HEADS, D_MODEL, D_MODEL)
N_IN = HG_DIM * 2 + HG_VDIM * 2 + SSM_INNER + SSM_XBC + SSM_HEADS + D_MODEL * 2

kernel_name = "hgrn2_mamba2_gated_parallel_deepnorm_stream_step"


def split_cols(a, sizes):
    out, start = [], 0
    for s in sizes:
        out.append(a[..., start:start + s])
        start += s
    return out


def layer_norm(x, g, b):
    xf = x.astype(jnp.float32)
    mu = jnp.mean(xf, axis=-1, keepdims=True)
    var = jnp.mean(jnp.square(xf - mu), axis=-1, keepdims=True)
    return ((xf - mu) * lax.rsqrt(var + LN_EPS) * g.astype(jnp.float32) + b.astype(jnp.float32)).astype(x.dtype)


def rms_norm_groups(x, n_groups, w):
    shp = x.shape
    xf = x.astype(jnp.float32).reshape(shp[:-1] + (n_groups, shp[-1] // n_groups))
    xf = xf * lax.rsqrt(jnp.mean(jnp.square(xf), axis=-1, keepdims=True) + RMS_EPS)
    return xf.reshape(shp) * w.astype(jnp.float32)


def causal_dwconv(x, buf, w, b):
    width = w.shape[0]
    t = x.shape[1]
    xp = jnp.concatenate([buf.astype(x.dtype), x], axis=1)
    out = b + sum(xp[:, k:k + t] * w[k] for k in range(width))
    return out, xp[:, xp.shape[1] - (width - 1):]


def pad_blocks(a, block):
    bsz, t = a.shape[0], a.shape[1]
    nb = -(-t // block)
    a = jnp.pad(a, [(0, 0), (0, nb * block - t)] + [(0, 0)] * (a.ndim - 2))
    return jnp.moveaxis(a.reshape((bsz, nb, block) + a.shape[2:]), 1, 0)


def unblock(a, t):
    a = jnp.moveaxis(a, 0, 1)
    return a.reshape((a.shape[0], a.shape[1] * a.shape[2]) + a.shape[3:])[:, :t]


def gla_chunked(q, k, v, log_f, s0):
    t = q.shape[1]
    f32 = jnp.float32
    blocks = tuple(pad_blocks(a.astype(f32), GLA_BLOCK) for a in (q, k, v, log_f))
    causal = jnp.tril(jnp.ones((GLA_BLOCK, GLA_BLOCK), bool))[None, :, :, None, None]

    def step(s, blk):
        qb, kb, vb, gb = blk
        cum = jnp.cumsum(gb, axis=1)
        diff = cum[:, :, None] - cum[:, None, :]
        dec = jnp.where(causal, jnp.exp(jnp.where(causal, diff, 0.0)), 0.0)
        scores = jnp.sum(qb[:, :, None] * kb[:, None, :] * dec, axis=-1)
        o = (jnp.einsum('btsh,bshv->bthv', scores, vb)
             + jnp.einsum('bthd,bhdv->bthv', qb * jnp.exp(cum), s))
        c_last = cum[:, -1]
        k_dec = kb * jnp.exp(c_last[:, None] - cum)
        s = jnp.exp(c_last)[..., None] * s + jnp.einsum('bshd,bshv->bhdv', k_dec, vb)
        return s, o

    s_fin, o = lax.scan(step, s0.astype(f32), blocks)
    return unblock(o, t).astype(v.dtype), s_fin.astype(s0.dtype)


def ssd_chunked(x, dt, a_neg, bm, cm, h0):
    bsz, t = x.shape[0], x.shape[1]
    f32 = jnp.float32
    xg = x.astype(f32).reshape(bsz, t, SSM_GROUPS, SSM_HPG, SSM_HEAD_DIM)
    dtg = dt.astype(f32).reshape(bsz, t, SSM_GROUPS, SSM_HPG)
    a_g = a_neg.astype(f32).reshape(SSM_GROUPS, SSM_HPG)
    blocks = tuple(pad_blocks(a, SSD_BLOCK) for a in (xg, dtg, bm.astype(f32), cm.astype(f32)))
    causal = jnp.tril(jnp.ones((SSD_BLOCK, SSD_BLOCK), bool))[None, :, :, None, None]

    def step(h, blk):
        xb, dtb, bb, cb = blk
        cum = jnp.cumsum(dtb * a_g, axis=1)
        diff = cum[:, :, None] - cum[:, None, :]
        lmat = jnp.where(causal, jnp.exp(jnp.where(causal, diff, 0.0)), 0.0)
        cbm = jnp.einsum('btgn,bsgn->btsg', cb, bb)
        scores = cbm[..., None] * lmat * dtb[:, None]
        y = (jnp.einsum('btsgj,bsgjp->btgjp', scores, xb)
             + jnp.einsum('btgn,bgjpn->btgjp', cb, h) * jnp.exp(cum)[..., None])
        c_last = cum[:, -1]
        wgt = jnp.exp(c_last[:, None] - cum) * dtb
        h = jnp.exp(c_last)[..., None, None] * h + jnp.einsum('bsgj,bsgn,bsgjp->bgjpn', wgt, bb, xb)
        return h, y

    h_init = h0.astype(f32).reshape(bsz, SSM_GROUPS, SSM_HPG, SSM_HEAD_DIM, SSM_STATE)
    h_fin, y = lax.scan(step, h_init, blocks)
    y = unblock(y, t).reshape(bsz, t, SSM_HEADS, SSM_HEAD_DIM)
    return y, h_fin.reshape(bsz, SSM_HEADS, SSM_HEAD_DIM, SSM_STATE).astype(h0.dtype)


def parallel_mixer(x, s_hg, h_ssm, conv_buf, lb, w_in, hg_norm_w, w_proj_a, ssm_conv_w, ssm_conv_b,
                   ssm_dt_bias, ssm_a_log, ssm_d, ssm_norm_w, w_proj_b, w_out):
    bsz, t, _ = x.shape
    f32 = jnp.float32
    proj = x @ w_in
    q, f_raw, i_in, g_out, z, xbc, dt_raw, gate_a, gate_b = split_cols(proj, SPLIT_SIZES)

    qh = jax.nn.silu(q).reshape(bsz, t, HG_HEADS, HG_DK)
    lbf = lb.astype(f32)
    f = lbf + (1.0 - lbf) * jax.nn.sigmoid(f_raw.astype(f32))
    log_f = jnp.log(f).reshape(bsz, t, HG_HEADS, HG_DK)
    kh = (1.0 - f).reshape(bsz, t, HG_HEADS, HG_DK)
    vh = i_in.reshape(bsz, t, HG_HEADS, HG_DV)
    o, s_hg_new = gla_chunked(qh, kh, vh, log_f, s_hg)
    o = rms_norm_groups(o.reshape(bsz, t, HG_VDIM), HG_HEADS, hg_norm_w) * jax.nn.silu(g_out.astype(f32))
    u_a = o.astype(x.dtype) @ w_proj_a

    xbc, conv_new = causal_dwconv(xbc, conv_buf, ssm_conv_w, ssm_conv_b)
    xbc = jax.nn.silu(xbc)
    xs, bm, cm = split_cols(xbc, (SSM_INNER, SSM_GROUPS * SSM_STATE, SSM_GROUPS * SSM_STATE))
    xs = xs.reshape(bsz, t, SSM_HEADS, SSM_HEAD_DIM)
    bm = bm.reshape(bsz, t, SSM_GROUPS, SSM_STATE)
    cm = cm.reshape(bsz, t, SSM_GROUPS, SSM_STATE)
    dt = jax.nn.softplus(dt_raw.astype(f32) + ssm_dt_bias.astype(f32))
    a_neg = -jnp.exp(ssm_a_log.astype(f32))
    y, h_new = ssd_chunked(xs, dt, a_neg, bm, cm, h_ssm)
    y = y + ssm_d.astype(f32)[:, None] * xs.astype(f32)
    y = rms_norm_groups(y.reshape(bsz, t, SSM_INNER) * jax.nn.silu(z.astype(f32)), SSM_GROUPS, ssm_norm_w)
    u_b = y.astype(x.dtype) @ w_proj_b

    merged = jax.nn.sigmoid(gate_a) * u_a + jax.nn.sigmoid(gate_b) * u_b
    return merged @ w_out, s_hg_new, h_new, conv_new


def conv_ffn(x, buf, w_up, conv_w, conv_b, w_down):
    a, v = split_cols(x @ w_up, (D_FF, D_FF))
    a, buf_new = causal_dwconv(a, buf, conv_w, conv_b)
    return (jax.nn.gelu(a, approximate=False) * v) @ w_down, buf_new


def run_trunk(x, st_hg, st_ssm, st_conv, st_ffn, lbs, w_in, hgrn_norm_w, w_proj_a, ssm_conv_w, ssm_conv_b,
              ssm_dt_bias, ssm_a_log, ssm_d, ssm_norm_w, w_proj_b, w_out, ln1_g, ln1_b,
              ffn_w_up, ffn_conv_w, ffn_conv_b, ffn_w_down, ln2_g, ln2_b):
    new_hg, new_ssm, new_conv, new_ffn = [], [], [], []
    for l in range(DEPTH):
        m, s_hg, s_ssm, s_conv = parallel_mixer(
            x, st_hg[l], st_ssm[l], st_conv[l], lbs[l], w_in[l], hgrn_norm_w[l], w_proj_a[l],
            ssm_conv_w[l], ssm_conv_b[l], ssm_dt_bias[l], ssm_a_log[l], ssm_d[l], ssm_norm_w[l],
            w_proj_b[l], w_out[l])
        x = layer_norm(ALPHA * x + m, ln1_g[l], ln1_b[l])
        fo, s_ffn = conv_ffn(x, st_ffn[l], ffn_w_up[l], ffn_conv_w[l], ffn_conv_b[l], ffn_w_down[l])
        x = layer_norm(ALPHA * x + fo, ln2_g[l], ln2_b[l])
        new_hg.append(s_hg)
        new_ssm.append(s_ssm)
        new_conv.append(s_conv)
        new_ffn.append(s_ffn)
    return x, jnp.stack(new_hg), jnp.stack(new_ssm), jnp.stack(new_conv), jnp.stack(new_ffn)


def setup_inputs(seed: int = 0) -> dict:
    key = jax.random.key(seed)
    ks = jax.random.split(key, 32)
    nrm = jax.random.normal
    f32 = jnp.float32
    dt0 = jnp.exp(jax.random.uniform(ks[12], (DEPTH, SSM_HEADS), f32, np.log(1e-3), np.log(1e-1)))
    return {
        "x_prompt": nrm(ks[0], (BATCH, SEQ, D_MODEL), f32),
        "x_sample": nrm(ks[1], (DEC_BATCH, DEC_SEQ, D_MODEL), f32),
        "state_hgrn": 0.3 * nrm(ks[2], (DEPTH, DEC_BATCH, HG_HEADS, HG_DK, HG_DV), f32),
        "state_ssm": 0.1 * nrm(ks[3], (DEPTH, DEC_BATCH, SSM_HEADS, SSM_HEAD_DIM, SSM_STATE), f32),
        "state_ssm_conv": nrm(ks[4], (DEPTH, DEC_BATCH, SSM_CONV - 1, SSM_XBC), f32),
        "state_ffn_conv": nrm(ks[5], (DEPTH, DEC_BATCH, FFN_CONV - 1, D_FF), f32),
        "meta_tokens": nrm(ks[6], (N_META, D_MODEL), f32),
        "w_in": nrm(ks[7], (DEPTH, D_MODEL, N_IN), f32) * D_MODEL ** -0.5,
        "hgrn_lb_logits": nrm(ks[8], (DEPTH, HG_DIM), f32),
        "hgrn_norm_w": 1.0 + 0.02 * nrm(ks[9], (DEPTH, HG_VDIM), f32),
        "w_proj_a": nrm(ks[10], (DEPTH, HG_VDIM, D_MODEL), f32) * HG_VDIM ** -0.5,
        "ssm_conv_w": nrm(ks[11], (DEPTH, SSM_CONV, SSM_XBC), f32) * SSM_CONV ** -0.5,
        "ssm_conv_b": 0.02 * nrm(ks[13], (DEPTH, SSM_XBC), f32),
        "ssm_dt_bias": dt0 + jnp.log(-jnp.expm1(-dt0)),
        "ssm_a_log": jnp.log(jax.random.uniform(ks[14], (DEPTH, SSM_HEADS), f32, 1.0, 16.0)),
        "ssm_d": 1.0 + 0.1 * nrm(ks[15], (DEPTH, SSM_HEADS), f32),
        "ssm_norm_w": 1.0 + 0.02 * nrm(ks[16], (DEPTH, SSM_INNER), f32),
        "w_proj_b": nrm(ks[17], (DEPTH, SSM_INNER, D_MODEL), f32) * SSM_INNER ** -0.5,
        "w_out": nrm(ks[18], (DEPTH, D_MODEL, D_MODEL), f32) * (D_MODEL ** -0.5 * BETA),
        "ln1_g": 1.0 + 0.02 * nrm(ks[19], (DEPTH, D_MODEL), f32),
        "ln1_b": 0.02 * nrm(ks[20], (DEPTH, D_MODEL), f32),
        "ffn_w_up": nrm(ks[21], (DEPTH, D_MODEL, 2 * D_FF), f32) * D_MODEL ** -0.5,
        "ffn_conv_w": nrm(ks[22], (DEPTH, FFN_CONV, D_FF), f32) * FFN_CONV ** -0.5,
        "ffn_conv_b": 0.02 * nrm(ks[23], (DEPTH, D_FF), f32),
        "ffn_w_down": nrm(ks[24], (DEPTH, D_FF, D_MODEL), f32) * (D_FF ** -0.5 * BETA),
        "ln2_g": 1.0 + 0.02 * nrm(ks[25], (DEPTH, D_MODEL), f32),
        "ln2_b": 0.02 * nrm(ks[26], (DEPTH, D_MODEL), f32),
    }


def reference(x_prompt, x_sample, state_hgrn, state_ssm, state_ssm_conv, state_ffn_conv, meta_tokens,
              w_in, hgrn_lb_logits, hgrn_norm_w, w_proj_a, ssm_conv_w, ssm_conv_b, ssm_dt_bias, ssm_a_log,
              ssm_d, ssm_norm_w, w_proj_b, w_out, ln1_g, ln1_b, ffn_w_up, ffn_conv_w, ffn_conv_b,
              ffn_w_down, ln2_g, ln2_b):
    sm = jax.nn.softmax(hgrn_lb_logits.astype(jnp.float32), axis=0)
    lbs = jnp.cumsum(sm, axis=0) - sm[0]
    weights = (lbs, w_in, hgrn_norm_w, w_proj_a, ssm_conv_w, ssm_conv_b, ssm_dt_bias, ssm_a_log, ssm_d,
               ssm_norm_w, w_proj_b, w_out, ln1_g, ln1_b, ffn_w_up, ffn_conv_w, ffn_conv_b, ffn_w_down,
               ln2_g, ln2_b)

    dt_ = x_prompt.dtype
    xp = jnp.concatenate([jnp.broadcast_to(meta_tokens.astype(dt_), (BATCH, N_META, D_MODEL)), x_prompt], axis=1)
    z_hg = jnp.zeros((DEPTH, BATCH, HG_HEADS, HG_DK, HG_DV), dt_)
    z_ssm = jnp.zeros((DEPTH, BATCH, SSM_HEADS, SSM_HEAD_DIM, SSM_STATE), dt_)
    z_conv = jnp.zeros((DEPTH, BATCH, SSM_CONV - 1, SSM_XBC), dt_)
    z_ffn = jnp.zeros((DEPTH, BATCH, FFN_CONV - 1, D_FF), dt_)
    yp, hg_p, ssm_p, conv_p, ffn_p = run_trunk(xp, z_hg, z_ssm, z_conv, z_ffn, *weights)
    y_prompt = yp[:, N_META:]

    y_sample, hg_s, ssm_s, conv_s, ffn_s = run_trunk(x_sample, state_hgrn, state_ssm, state_ssm_conv,
                                                     state_ffn_conv, *weights)
    return (y_prompt, y_sample, hg_p, ssm_p, conv_p, ffn_p, hg_s, ssm_s, conv_s, ffn_s)
```

```python
import functools
import math

import jax
import jax.numpy as jnp
from jax import lax
from jax.experimental import pallas as pl
from jax.experimental.pallas import tpu as pltpu

F32 = jnp.float32
BF16 = jnp.bfloat16
HIGHEST = lax.Precision.HIGHEST

D_MODEL = 2048
DEPTH = 2
N_META = 16
HG_HEADS = 16
HG_D = 128
SSM_HEADS = 32
SSM_P = 64
SSM_GROUPS = 4
SSM_HPG = SSM_HEADS // SSM_GROUPS
SSM_N = 128
SSM_INNER = SSM_HEADS * SSM_P
SSM_GW = SSM_HPG * SSM_P
SSM_CONV = 4
SSM_XBC = SSM_INNER + 2 * SSM_GROUPS * SSM_N
D_FF = 5632
FFN_CONV = 3
ALPHA = (2.0 * DEPTH) ** 0.25
LN_EPS = 1e-5
RMS_EPS = 1e-6

LANES = 128
SUBLANES = 8
CHUNK = 128
N_LEVELS = int(math.log2(CHUNK))
VMEM_LIMIT = 48 * 1024 * 1024

COL_Q, COL_F, COL_I, COL_G, COL_Z = 0, 2048, 4096, 6144, 8192
COL_XS, COL_BM, COL_CM = 10240, 12288, 12800
COL_GA, COL_GB = 13312, 15360
N_MAIN = 17408
DT_LO, DT_HI = 13312, 13344


def _params(*semantics):
    return pltpu.CompilerParams(dimension_semantics=semantics, vmem_limit_bytes=VMEM_LIMIT)


def _nt(a, b):
    return lax.dot_general(a, b, (((1,), (1,)), ((), ())), preferred_element_type=F32)


def _split_bf16(x):
    hi = x.astype(BF16)
    lo = (x - hi.astype(F32)).astype(BF16)
    return hi, lo


def _nt_3pass(a, b):
    ah, al = _split_bf16(a)
    bh, bl = _split_bf16(b)
    return _nt(jnp.concatenate([ah, al, ah], axis=1), jnp.concatenate([bh, bh, bl], axis=1))


def _silu(x):
    return x * jax.nn.sigmoid(x)


def _layer_norm(h, g, b):
    mu = jnp.mean(h, axis=-1, keepdims=True)
    d = h - mu
    var = jnp.mean(d * d, axis=-1, keepdims=True)
    return d * lax.rsqrt(var + LN_EPS) * g + b


def _pad_rows(x, rows):
    if x.shape[0] == rows:
        return x
    return jnp.concatenate([x, jnp.zeros((rows - x.shape[0],) + x.shape[1:], x.dtype)], axis=0)


def _mm_kernel(x_ref, w_ref, o_ref):
    o_ref[...] = jnp.dot(x_ref[...], w_ref[...], preferred_element_type=F32).astype(o_ref.dtype)


def _matmul(x, w, tm, tn, out_dtype, name):
    m, k = x.shape
    n = w.shape[1]
    return pl.pallas_call(
        _mm_kernel,
        grid=(n // tn, m // tm),
        in_specs=[pl.BlockSpec((tm, k), lambda j, i: (i, 0)),
                  pl.BlockSpec((k, tn), lambda j, i: (0, j))],
        out_specs=pl.BlockSpec((tm, tn), lambda j, i: (i, j)),
        out_shape=jax.ShapeDtypeStruct((m, n), out_dtype),
        compiler_params=_params("arbitrary", "arbitrary"),
        name=name,
    )(x, w)


def _dt_kernel(x_ref, w_ref, b_ref, o_ref):
    raw = jnp.dot(x_ref[...], w_ref[...], preferred_element_type=F32) + b_ref[...]
    o_ref[...] = jnp.maximum(raw, 0.0) + jnp.log1p(jnp.exp(-jnp.abs(raw)))


def _dt_proj(xb, w_dt, dt_bias, tm):
    m, k = xb.shape
    n = w_dt.shape[1]
    return pl.pallas_call(
        _dt_kernel,
        grid=(m // tm,),
        in_specs=[pl.BlockSpec((tm, k), lambda i: (i, 0)),
                  pl.BlockSpec((k, n), lambda i: (0, 0)),
                  pl.BlockSpec((1, n), lambda i: (0, 0))],
        out_specs=pl.BlockSpec((tm, n), lambda i: (i, 0)),
        out_shape=jax.ShapeDtypeStruct((m, n), F32),
        compiler_params=_params("arbitrary"),
        name="dt_proj",
    )(xb, w_dt, dt_bias)


def _level_boundary(cum, h):
    c = cum.shape[0]
    if h >= SUBLANES:
        blk = cum.reshape(c // (2 * h), 2 * h, LANES)
        return jnp.broadcast_to(blk[:, h - 1:h, :], blk.shape).reshape(c, LANES)
    tiles = cum.reshape(c // SUBLANES, SUBLANES, LANES)
    sub = lax.broadcasted_iota(jnp.int32, tiles.shape, 1)
    out = None
    for start in range(0, SUBLANES, 2 * h):
        row = jnp.broadcast_to(tiles[:, start + h - 1:start + h, :], tiles.shape)
        out = row if out is None else jnp.where(sub >= start, row, out)
    return out.reshape(c, LANES)


def _gla_kernel(q_ref, f_ref, i_ref, g_ref, lbl_ref, nw_ref, s0_ref, tri_ref, msk_ref,
                o_ref, s_out_ref, st_ref, *, layer, heads, rows):
    c = pl.program_id(2)
    last = pl.num_programs(2) - 1

    @pl.when(c == 0)
    def _():
        for h in range(heads):
            st_ref[h] = s0_ref[0, h].T

    logits = lbl_ref[...]
    ex = jnp.exp(logits - jnp.max(logits, axis=0, keepdims=True))
    sm = ex / jnp.sum(ex, axis=0, keepdims=True)
    lb_all = jnp.zeros((1, sm.shape[1]), F32)
    for l in range(1, layer + 1):
        lb_all = lb_all + sm[l:l + 1]

    tri = tri_ref[...]
    valid = lax.broadcasted_iota(jnp.int32, (CHUNK, LANES), 0) < rows
    for h in range(heads):
        cols = slice(h * HG_D, (h + 1) * HG_D)
        lb = lb_all[:, cols]
        q = _silu(_pad_rows(q_ref[:, cols], CHUNK))
        f = lb + (1.0 - lb) * jax.nn.sigmoid(_pad_rows(f_ref[:, cols], CHUNK))
        v = _pad_rows(i_ref[:, cols], CHUNK)
        k = 1.0 - f
        g = jnp.log(f)
        if rows < CHUNK:
            g = jnp.where(valid, g, 0.0)
        cum = jnp.dot(tri, g, precision=HIGHEST, preferred_element_type=F32)

        a = _nt_3pass(q, k) * msk_ref[N_LEVELS]
        for lev in range(N_LEVELS):
            z = jnp.exp(-jnp.abs(cum - _level_boundary(cum, 1 << lev)))
            a = a + _nt_3pass(q * z, k * z) * msk_ref[lev]

        st = st_ref[h]
        o = (jnp.dot(a.astype(BF16), v.astype(BF16), preferred_element_type=F32)
             + _nt((q * jnp.exp(cum)).astype(BF16), st.astype(BF16)))
        c_last = cum[CHUNK - 1:CHUNK, :]
        k_dec = k * jnp.exp(c_last - cum)
        st_ref[h] = st * jnp.exp(c_last) + jnp.dot(v.T.astype(BF16), k_dec.astype(BF16),
                                                   preferred_element_type=F32)

        o = o[:rows]
        o = o * lax.rsqrt(jnp.mean(o * o, axis=-1, keepdims=True) + RMS_EPS) * nw_ref[:, cols]
        o_ref[:, cols] = (o * _silu(g_ref[:, cols])).astype(o_ref.dtype)

    @pl.when(c == last)
    def _():
        for h in range(heads):
            s_out_ref[0, h] = st_ref[h].T


def _gla(proj, lb_logits, norm_w, s0, consts, *, layer, batch, seq, heads_per_step=4):
    rows = min(seq, CHUNK)
    nc = seq // rows
    hp = heads_per_step
    w = hp * HG_D
    m = batch * seq

    def col(base):
        return lambda b, hb, c: (b * nc + c, base // w + hb)

    par = lambda b, hb, c: (0, hb)
    state = lambda b, hb, c: (b, hb, 0, 0)
    return pl.pallas_call(
        functools.partial(_gla_kernel, layer=layer, heads=hp, rows=rows),
        grid=(batch, HG_HEADS // hp, nc),
        in_specs=[pl.BlockSpec((rows, w), col(COL_Q)),
                  pl.BlockSpec((rows, w), col(COL_F)),
                  pl.BlockSpec((rows, w), col(COL_I)),
                  pl.BlockSpec((rows, w), col(COL_G)),
                  pl.BlockSpec((DEPTH, w), par),
                  pl.BlockSpec((1, w), par),
                  pl.BlockSpec((1, hp, HG_D, HG_D), state),
                  pl.BlockSpec((CHUNK, CHUNK), lambda b, hb, c: (0, 0)),
                  pl.BlockSpec((N_LEVELS + 1, CHUNK, CHUNK), lambda b, hb, c: (0, 0, 0))],
        out_specs=[pl.BlockSpec((rows, w), lambda b, hb, c: (b * nc + c, hb)),
                   pl.BlockSpec((1, hp, HG_D, HG_D), state)],
        out_shape=[jax.ShapeDtypeStruct((m, HG_HEADS * HG_D), BF16),
                   jax.ShapeDtypeStruct((batch, HG_HEADS, HG_D, HG_D), F32)],
        scratch_shapes=[pltpu.VMEM((hp, HG_D, HG_D), F32)],
        compiler_params=_params("arbitrary", "arbitrary", "arbitrary"),
        name="hgrn2",
    )(proj, proj, proj, proj, lb_logits, norm_w, s0, consts["tri"], consts["levels"])


def _causal_conv4(raw, prev_ref, w, bias):
    n = raw.shape[0]
    ext = jnp.concatenate([prev_ref[...], raw], axis=0)
    out = bias + w[3:4] * raw + w[2:3] * ext[7:7 + n] + w[1:2] * ext[6:6 + n] + w[0:1] * ext[5:5 + n]
    prev_ref[...] = raw[n - SUBLANES:]
    return out


def _ssd_kernel(xs_ref, bm_ref, cm_ref, z_ref, dt_ref, cwx_ref, cwb_ref, cwc_ref, cbx_ref, cbb_ref, cbc_ref,
                stx_ref, stb_ref, stc_ref, alog_ref, dskip_ref, nw_ref, h0_ref, tri_ref, exp_ref,
                y_ref, h_out_ref, stx_out_ref, stb_out_ref, stc_out_ref,
                ht_ref, px_ref, pb_ref, pc_ref, *, rows):
    c = pl.program_id(2)
    last = pl.num_programs(2) - 1

    @pl.when(c == 0)
    def _():
        ht_ref[...] = h0_ref[0].T
        for prev, st in ((px_ref, stx_ref), (pb_ref, stb_ref), (pc_ref, stc_ref)):
            buf = st[0]
            prev[...] = jnp.concatenate(
                [jnp.zeros((SUBLANES - (SSM_CONV - 1), buf.shape[1]), F32), buf], axis=0)

    raw_x, raw_b, raw_c = xs_ref[...], bm_ref[...], cm_ref[...]
    stx_out_ref[0] = raw_x[rows - (SSM_CONV - 1):]
    stb_out_ref[0] = raw_b[rows - (SSM_CONV - 1):]
    stc_out_ref[0] = raw_c[rows - (SSM_CONV - 1):]
    xs = _pad_rows(_silu(_causal_conv4(raw_x, px_ref, cwx_ref[...], cbx_ref[...])), CHUNK)
    bm = _pad_rows(_silu(_causal_conv4(raw_b, pb_ref, cwb_ref[...], cbb_ref[...])), CHUNK)
    cm = _pad_rows(_silu(_causal_conv4(raw_c, pc_ref, cwc_ref[...], cbc_ref[...])), CHUNK)

    dt = _pad_rows(dt_ref[...], CHUNK)
    if rows < CHUNK:
        dt = jnp.where(lax.broadcasted_iota(jnp.int32, dt.shape, 0) < rows, dt, 0.0)
    a = dt * (-jnp.exp(alog_ref[...]))
    cum = jnp.dot(tri_ref[...], a, precision=HIGHEST, preferred_element_type=F32)
    cum_t = cum.T
    dt_t = dt.T
    c_last = cum[CHUNK - 1:CHUNK, :]
    expand = exp_ref[...]

    def widen(per_head):
        return jnp.dot(per_head, expand, precision=HIGHEST, preferred_element_type=F32)

    ecum_x = widen(jnp.exp(cum))
    wgt_x = widen(jnp.exp(c_last - cum) * dt)
    tail = widen(jnp.concatenate([jnp.broadcast_to(jnp.exp(c_last), (SUBLANES // 2, LANES)),
                                  jnp.broadcast_to(dskip_ref[...], (SUBLANES // 2, LANES))], axis=0))
    dec_x, dskip_x = tail[0:1], tail[SUBLANES // 2:SUBLANES // 2 + 1]

    cb = _nt(cm.astype(BF16), bm.astype(BF16))
    ti = lax.broadcasted_iota(jnp.int32, (CHUNK, CHUNK), 0)
    si = lax.broadcasted_iota(jnp.int32, (CHUNK, CHUNK), 1)
    causal = ti >= si
    lane = lax.broadcasted_iota(jnp.int32, (CHUNK, LANES), 1)
    y_parts = []
    for pair in range(SSM_HPG // 2):
        xp = xs[:, pair * LANES:(pair + 1) * LANES]
        scores = []
        for j in (2 * pair, 2 * pair + 1):
            diff = cum[:, j:j + 1] - cum_t[j:j + 1, :]
            lmat = jnp.where(causal, jnp.exp(jnp.where(causal, diff, 0.0)), 0.0)
            scores.append((cb * lmat * dt_t[j:j + 1, :]).astype(BF16))
        lhs = jnp.concatenate(scores, axis=1)
        rhs = jnp.concatenate([jnp.where(lane < SSM_P, xp, 0.0).astype(BF16),
                               jnp.where(lane >= SSM_P, xp, 0.0).astype(BF16)], axis=0)
        y_parts.append(jnp.dot(lhs, rhs, preferred_element_type=F32))
    y = jnp.concatenate(y_parts, axis=1)

    ht = ht_ref[...]
    y = y + jnp.dot(cm.astype(BF16), ht.astype(BF16), preferred_element_type=F32) * ecum_x
    ht_ref[...] = ht * dec_x + jnp.dot(bm.T.astype(BF16), (xs * wgt_x).astype(BF16),
                                       preferred_element_type=F32)

    y = ((y + dskip_x * xs)[:rows]) * _silu(z_ref[...])
    y = y * lax.rsqrt(jnp.mean(y * y, axis=-1, keepdims=True) + RMS_EPS) * nw_ref[...]
    y_ref[...] = y.astype(y_ref.dtype)

    @pl.when(c == last)
    def _():
        h_out_ref[0] = ht_ref[...].T


def _ssd(proj, dt, p, h0, conv0, consts, *, batch, seq):
    rows = min(seq, CHUNK)
    nc = seq // rows
    m = batch * seq
    gw, n = SSM_GW, SSM_N
    tail = SSM_CONV - 1
    st_x, st_b, st_c = conv0[..., :SSM_INNER], conv0[..., SSM_INNER:SSM_INNER + SSM_GROUPS * n], \
        conv0[..., SSM_INNER + SSM_GROUPS * n:]
    h0 = h0.reshape(batch, SSM_INNER, n)

    def col(base, width):
        return lambda b, g, c: (b * nc + c, base // width + g)

    par = lambda b, g, c: (0, g)
    st = lambda b, g, c: (b, 0, g)
    const = lambda b, g, c: (0, 0)
    y, h_out, ox, ob, oc = pl.pallas_call(
        functools.partial(_ssd_kernel, rows=rows),
        grid=(batch, SSM_GROUPS, nc),
        in_specs=[pl.BlockSpec((rows, gw), col(COL_XS, gw)),
                  pl.BlockSpec((rows, n), col(COL_BM, n)),
                  pl.BlockSpec((rows, n), col(COL_CM, n)),
                  pl.BlockSpec((rows, gw), col(COL_Z, gw)),
                  pl.BlockSpec((rows, LANES), col(0, LANES)),
                  pl.BlockSpec((SSM_CONV, gw), par),
                  pl.BlockSpec((SSM_CONV, n), par),
                  pl.BlockSpec((SSM_CONV, n), par),
                  pl.BlockSpec((1, gw), par),
                  pl.BlockSpec((1, n), par),
                  pl.BlockSpec((1, n), par),
                  pl.BlockSpec((1, tail, gw), st),
                  pl.BlockSpec((1, tail, n), st),
                  pl.BlockSpec((1, tail, n), st),
                  pl.BlockSpec((1, LANES), par),
                  pl.BlockSpec((1, LANES), par),
                  pl.BlockSpec((1, gw), par),
                  pl.BlockSpec((1, gw, n), lambda b, g, c: (b, g, 0)),
                  pl.BlockSpec((CHUNK, CHUNK), const),
                  pl.BlockSpec((LANES, gw), const)],
        out_specs=[pl.BlockSpec((rows, gw), lambda b, g, c: (b * nc + c, g)),
                   pl.BlockSpec((1, gw, n), lambda b, g, c: (b, g, 0)),
                   pl.BlockSpec((1, tail, gw), st),
                   pl.BlockSpec((1, tail, n), st),
                   pl.BlockSpec((1, tail, n), st)],
        out_shape=[jax.ShapeDtypeStruct((m, SSM_INNER), BF16),
                   jax.ShapeDtypeStruct((batch, SSM_INNER, n), F32),
                   jax.ShapeDtypeStruct((batch, tail, SSM_INNER), F32),
                   jax.ShapeDtypeStruct((batch, tail, SSM_GROUPS * n), F32),
                   jax.ShapeDtypeStruct((batch, tail, SSM_GROUPS * n), F32)],
        scratch_shapes=[pltpu.VMEM((n, gw), F32),
                        pltpu.VMEM((SUBLANES, gw), F32),
                        pltpu.VMEM((SUBLANES, n), F32),
                        pltpu.VMEM((SUBLANES, n), F32)],
        compiler_params=_params("arbitrary", "arbitrary", "arbitrary"),
        name="ssd",
    )(proj, proj, proj, proj, dt, p["cw_x"], p["cw_b"], p["cw_c"], p["cb_x"], p["cb_b"], p["cb_c"],
      st_x, st_b, st_c, p["a_log"], p["d_skip"], p["ssm_norm_w"], h0, consts["tri"], consts["expand"])
    return (y, h_out.reshape(batch, SSM_HEADS, SSM_P, n), jnp.concatenate([ox, ob, oc], axis=-1))


def _merge_kernel(oa_ref, ob_ref, wa_ref, wb_ref, ga_ref, gb_ref, o_ref):
    ua = jnp.dot(oa_ref[...], wa_ref[...], preferred_element_type=F32)
    ub = jnp.dot(ob_ref[...], wb_ref[...], preferred_element_type=F32)
    o_ref[...] = (jax.nn.sigmoid(ga_ref[...]) * ua + jax.nn.sigmoid(gb_ref[...]) * ub).astype(o_ref.dtype)


def _merge(o_a, o_b, w_a, w_b, proj, tm, tn=512):
    m, k = o_a.shape
    return pl.pallas_call(
        _merge_kernel,
        grid=(D_MODEL // tn, m // tm),
        in_specs=[pl.BlockSpec((tm, k), lambda j, i: (i, 0)),
                  pl.BlockSpec((tm, k), lambda j, i: (i, 0)),
                  pl.BlockSpec((k, tn), lambda j, i: (0, j)),
                  pl.BlockSpec((k, tn), lambda j, i: (0, j)),
                  pl.BlockSpec((tm, tn), lambda j, i: (i, COL_GA // tn + j)),
                  pl.BlockSpec((tm, tn), lambda j, i: (i, COL_GB // tn + j))],
        out_specs=pl.BlockSpec((tm, tn), lambda j, i: (i, j)),
        out_shape=jax.ShapeDtypeStruct((m, D_MODEL), BF16),
        compiler_params=_params("arbitrary", "arbitrary"),
        name="merge",
    )(o_a, o_b, w_a, w_b, proj, proj)


def _out_ln_kernel(mg_ref, w_ref, x_ref, g_ref, b_ref, o_ref, ob_ref):
    h = ALPHA * x_ref[...] + jnp.dot(mg_ref[...], w_ref[...], preferred_element_type=F32)
    y = _layer_norm(h, g_ref[...], b_ref[...])
    o_ref[...] = y
    ob_ref[...] = y.astype(ob_ref.dtype)


def _out_ln(merged, w_out, x, g, b, tm):
    m, k = merged.shape
    row = lambda i: (i, 0)
    fix = lambda i: (0, 0)
    return pl.pallas_call(
        _out_ln_kernel,
        grid=(m // tm,),
        in_specs=[pl.BlockSpec((tm, k), row),
                  pl.BlockSpec((k, D_MODEL), fix),
                  pl.BlockSpec((tm, D_MODEL), row),
                  pl.BlockSpec((1, D_MODEL), fix),
                  pl.BlockSpec((1, D_MODEL), fix)],
        out_specs=[pl.BlockSpec((tm, D_MODEL), row), pl.BlockSpec((tm, D_MODEL), row)],
        out_shape=[jax.ShapeDtypeStruct((m, D_MODEL), F32), jax.ShapeDtypeStruct((m, D_MODEL), BF16)],
        compiler_params=_params("arbitrary"),
        name="out_ln",
    )(merged, w_out, x, g, b)


def _ffn_up_kernel(x_ref, wa_ref, wv_ref, cw_ref, cb_ref, st_ref, h_ref, st_out_ref, prev_ref,
                   *, seqs, seq_rows, tiles_per_seq):
    i = pl.program_id(1)
    tn = wa_ref.shape[1]
    x = x_ref[...]
    a = jnp.dot(x, wa_ref[...], preferred_element_type=F32).reshape(seqs, seq_rows, tn)
    v = jnp.dot(x, wv_ref[...], preferred_element_type=F32).reshape(seqs, seq_rows, tn)

    @pl.when(i % tiles_per_seq == 0)
    def _():
        prev_ref[...] = jnp.concatenate(
            [jnp.zeros((seqs, SUBLANES - (FFN_CONV - 1), tn), F32), st_ref[...]], axis=1)

    ext = jnp.concatenate([prev_ref[...], a], axis=1)
    w = cw_ref[...]
    n = seq_rows
    conv = cb_ref[...] + w[2:3] * a + w[1:2] * ext[:, 7:7 + n] + w[0:1] * ext[:, 6:6 + n]
    prev_ref[...] = a[:, n - SUBLANES:]
    st_out_ref[...] = a[:, n - (FFN_CONV - 1):]
    gelu = 0.5 * conv * (1.0 + lax.erf(conv * (1.0 / math.sqrt(2.0))))
    h_ref[...] = (gelu * v).reshape(seqs * seq_rows, tn).astype(h_ref.dtype)


def _ffn_up(xb, w_up, conv_w, conv_b, st, *, batch, seq, tm, tn=512):
    m, k = xb.shape
    if tm >= seq:
        seqs, seq_rows, tps = tm // seq, seq, 1
    else:
        seqs, seq_rows, tps = 1, tm, seq // tm
    nj = D_FF // tn
    tail = FFN_CONV - 1
    st_idx = lambda j, i: (i // tps, 0, j)
    return pl.pallas_call(
        functools.partial(_ffn_up_kernel, seqs=seqs, seq_rows=seq_rows, tiles_per_seq=tps),
        grid=(nj, m // tm),
        in_specs=[pl.BlockSpec((tm, k), lambda j, i: (i, 0)),
                  pl.BlockSpec((k, tn), lambda j, i: (0, j)),
                  pl.BlockSpec((k, tn), lambda j, i: (0, nj + j)),
                  pl.BlockSpec((FFN_CONV, tn), lambda j, i: (0, j)),
                  pl.BlockSpec((1, tn), lambda j, i: (0, j)),
                  pl.BlockSpec((seqs, tail, tn), st_idx)],
        out_specs=[pl.BlockSpec((tm, tn), lambda j, i: (i, j)),
                   pl.BlockSpec((seqs, tail, tn), st_idx)],
        out_shape=[jax.ShapeDtypeStruct((m, D_FF), BF16),
                   jax.ShapeDtypeStruct((batch, tail, D_FF), F32)],
        scratch_shapes=[pltpu.VMEM((seqs, SUBLANES, tn), F32)],
        compiler_params=_params("arbitrary", "arbitrary"),
        name="ffn_up",
    )(xb, w_up, w_up, conv_w, conv_b, st)


def _ffn_down_kernel(h_ref, w_ref, x_ref, g_ref, b_ref, o_ref, ob_ref, acc_ref):
    kk = pl.program_id(1)

    @pl.when(kk == 0)
    def _():
        acc_ref[...] = ALPHA * x_ref[...]

    acc_ref[...] += jnp.dot(h_ref[...], w_ref[...], preferred_element_type=F32)

    @pl.when(kk == pl.num_programs(1) - 1)
    def _():
        y = _layer_norm(acc_ref[...], g_ref[...], b_ref[...])
        o_ref[...] = y
        ob_ref[...] = y.astype(ob_ref.dtype)


def _ffn_down(h, w_down, x, g, b, tm, tk=512):
    m, k = h.shape
    row = lambda i, kk: (i, 0)
    fix = lambda i, kk: (0, 0)
    return pl.pallas_call(
        _ffn_down_kernel,
        grid=(m // tm, k // tk),
        in_specs=[pl.BlockSpec((tm, tk), lambda i, kk: (i, kk)),
                  pl.BlockSpec((tk, D_MODEL), lambda i, kk: (kk, 0)),
                  pl.BlockSpec((tm, D_MODEL), row),
                  pl.BlockSpec((1, D_MODEL), fix),
                  pl.BlockSpec((1, D_MODEL), fix)],
        out_specs=[pl.BlockSpec((tm, D_MODEL), row), pl.BlockSpec((tm, D_MODEL), row)],
        out_shape=[jax.ShapeDtypeStruct((m, D_MODEL), F32), jax.ShapeDtypeStruct((m, D_MODEL), BF16)],
        scratch_shapes=[pltpu.VMEM((tm, D_MODEL), F32)],
        compiler_params=_params("arbitrary", "arbitrary"),
        name="ffn_down",
    )(h, w_down, x, g, b)


def _group_lanes(v):
    v = v.astype(F32).reshape(SSM_GROUPS, SSM_HPG)
    return jnp.pad(v, ((0, 0), (0, LANES - SSM_HPG))).reshape(1, SSM_GROUPS * LANES)


def _layer_params(l, w_in, hgrn_norm_w, w_proj_a, ssm_conv_w, ssm_conv_b, ssm_dt_bias, ssm_a_log, ssm_d,
                  ssm_norm_w, w_proj_b, w_out, ln1_g, ln1_b, ffn_w_up, ffn_conv_w, ffn_conv_b, ffn_w_down,
                  ln2_g, ln2_b):
    wi = w_in[l]
    w_dt = wi[:, DT_LO:DT_HI].reshape(D_MODEL, SSM_GROUPS, SSM_HPG)
    w_dt = jnp.pad(w_dt, ((0, 0), (0, 0), (0, LANES - SSM_HPG))).reshape(D_MODEL, SSM_GROUPS * LANES)
    cw, cb = ssm_conv_w[l], ssm_conv_b[l][None, :]
    gn = SSM_GROUPS * SSM_N
    row = lambda a: a[l][None, :].astype(F32)
    return dict(
        w_main=jnp.concatenate([wi[:, :DT_LO], wi[:, DT_HI:]], axis=1).astype(BF16),
        w_dt=w_dt.astype(BF16),
        dt_bias=_group_lanes(ssm_dt_bias[l]), a_log=_group_lanes(ssm_a_log[l]), d_skip=_group_lanes(ssm_d[l]),
        cw_x=cw[:, :SSM_INNER], cw_b=cw[:, SSM_INNER:SSM_INNER + gn], cw_c=cw[:, SSM_INNER + gn:],
        cb_x=cb[:, :SSM_INNER], cb_b=cb[:, SSM_INNER:SSM_INNER + gn], cb_c=cb[:, SSM_INNER + gn:],
        hgrn_norm_w=row(hgrn_norm_w), ssm_norm_w=row(ssm_norm_w),
        w_proj_a=w_proj_a[l].astype(BF16), w_proj_b=w_proj_b[l].astype(BF16), w_out=w_out[l].astype(BF16),
        ln1_g=row(ln1_g), ln1_b=row(ln1_b), ln2_g=row(ln2_g), ln2_b=row(ln2_b),
        w_up=ffn_w_up[l].astype(BF16), ffn_cw=ffn_conv_w[l], ffn_cb=row(ffn_conv_b),
        w_down=ffn_w_down[l].astype(BF16),
    )


def _constants():
    t = jnp.arange(CHUNK)[:, None]
    s = jnp.arange(CHUNK)[None, :]
    levels = [((t > s) & (((t ^ s) >> lev) == 1)).astype(F32) for lev in range(N_LEVELS)]
    levels.append((t == s).astype(F32))
    head = jnp.arange(LANES)[:, None]
    chan = jnp.arange(SSM_GW)[None, :]
    return dict(tri=(t >= s).astype(F32), levels=jnp.stack(levels),
                expand=(chan // SSM_P == head).astype(F32))


def _run_group(x, st_hg, st_ssm, st_conv, st_ffn, lb_logits, layers, consts, *, batch, seq):
    m = batch * seq
    tm = min(m, 512)
    xb = x.astype(BF16)
    new_hg, new_ssm, new_conv, new_ffn = [], [], [], []
    for l, p in enumerate(layers):
        proj = _matmul(xb, p["w_main"], tm, 1024, F32, "in_proj")
        dt = _dt_proj(xb, p["w_dt"], p["dt_bias"], tm)
        o_a, s_hg = _gla(proj, lb_logits, p["hgrn_norm_w"], st_hg[l], consts, layer=l, batch=batch, seq=seq)
        o_b, s_ssm, s_conv = _ssd(proj, dt, p, st_ssm[l], st_conv[l], consts, batch=batch, seq=seq)
        merged = _merge(o_a, o_b, p["w_proj_a"], p["w_proj_b"], proj, tm)
        x, xb = _out_ln(merged, p["w_out"], x, p["ln1_g"], p["ln1_b"], min(m, 256))
        hid, s_ffn = _ffn_up(xb, p["w_up"], p["ffn_cw"], p["ffn_cb"], st_ffn[l], batch=batch, seq=seq, tm=tm)
        x, xb = _ffn_down(hid, p["w_down"], x, p["ln2_g"], p["ln2_b"], tm)
        new_hg.append(s_hg)
        new_ssm.append(s_ssm)
        new_conv.append(s_conv)
        new_ffn.append(s_ffn)
    return x, new_hg, new_ssm, new_conv, new_ffn


def kernel(x_prompt, x_sample, state_hgrn, state_ssm, state_ssm_conv, state_ffn_conv, meta_tokens, w_in,
           hgrn_lb_logits, hgrn_norm_w, w_proj_a, ssm_conv_w, ssm_conv_b, ssm_dt_bias, ssm_a_log, ssm_d,
           ssm_norm_w, w_proj_b, w_out, ln1_g, ln1_b, ffn_w_up, ffn_conv_w, ffn_conv_b, ffn_w_down, ln2_g, ln2_b):
    batch, seq, _ = x_prompt.shape
    dec_batch, dec_seq, _ = x_sample.shape
    layers = [_layer_params(l, w_in, hgrn_norm_w, w_proj_a, ssm_conv_w, ssm_conv_b, ssm_dt_bias, ssm_a_log,
                            ssm_d, ssm_norm_w, w_proj_b, w_out, ln1_g, ln1_b, ffn_w_up, ffn_conv_w,
                            ffn_conv_b, ffn_w_down, ln2_g, ln2_b) for l in range(DEPTH)]
    consts = _constants()
    lb_logits = hgrn_lb_logits.astype(F32)
    run = functools.partial(_run_group, lb_logits=lb_logits, layers=layers, consts=consts)

    zeros = lambda *shape: [jnp.zeros(shape, F32)] * DEPTH
    _, hg_m, ssm_m, conv_m, ffn_m = run(
        meta_tokens.astype(F32), zeros(1, HG_HEADS, HG_D, HG_D), zeros(1, SSM_HEADS, SSM_P, SSM_N),
        zeros(1, SSM_CONV - 1, SSM_XBC), zeros(1, FFN_CONV - 1, D_FF), batch=1, seq=N_META)
    seed = lambda sts: [jnp.broadcast_to(s, (batch,) + s.shape[1:]) for s in sts]
    yp, hg_p, ssm_p, conv_p, ffn_p = run(
        x_prompt.reshape(batch * seq, D_MODEL), seed(hg_m), seed(ssm_m), seed(conv_m), seed(ffn_m),
        batch=batch, seq=seq)
    ys, hg_s, ssm_s, conv_s, ffn_s = run(
        x_sample.reshape(dec_batch * dec_seq, D_MODEL), state_hgrn, state_ssm, state_ssm_conv, state_ffn_conv,
        batch=dec_batch, seq=dec_seq)
    return (yp.reshape(batch, seq, D_MODEL), ys.reshape(dec_batch, dec_seq, D_MODEL),
            jnp.stack(hg_p), jnp.stack(ssm_p), jnp.stack(conv_p), jnp.stack(ffn_p),
            jnp.stack(hg_s), jnp.stack(ssm_s), jnp.stack(conv_s), jnp.stack(ffn_s))
```

```python
import functools
import math

import jax
import jax.numpy as jnp
from jax import lax
from jax.experimental import pallas as pl
from jax.experimental.pallas import tpu as pltpu

F32 = jnp.float32
BF16 = jnp.bfloat16

D_MODEL = 2048
DEPTH = 2
N_META = 16
HG_HEADS = 16
HG_D = 128
SSM_HEADS = 32
SSM_P = 64
SSM_GROUPS = 4
SSM_HPG = SSM_HEADS // SSM_GROUPS
SSM_N = 128
SSM_INNER = SSM_HEADS * SSM_P
SSM_GW = SSM_HPG * SSM_P
SSM_BC = 2 * SSM_GROUPS * SSM_N
SSM_CONV = 4
SSM_XBC = SSM_INNER + SSM_BC
D_FF = 5632
FFN_CONV = 3
ALPHA = (2.0 * DEPTH) ** 0.25
LN_EPS = 1e-5
RMS_EPS = 1e-6

LANES = 128
SUBLANES = 8
CHUNK = 128
N_LEVELS = int(math.log2(CHUNK))
TM = 512
TM_LN = 256
VMEM_LIMIT = 48 * 1024 * 1024
NEG_BIG = -1e30

COL_Q, COL_F, COL_I, COL_G, COL_Z = 0, 2048, 4096, 6144, 8192
COL_XS, COL_BC = 10240, 12288
N_MAIN = 13312
COL_DT = 13312
GATE_SHIFT = SSM_HEADS


def _params(*semantics):
    return pltpu.CompilerParams(dimension_semantics=semantics, vmem_limit_bytes=VMEM_LIMIT)


def _nt(a, b):
    return lax.dot_general(a, b, (((1,), (1,)), ((), ())), preferred_element_type=F32)


def _dot(a, b):
    return jnp.dot(a, b, preferred_element_type=F32)


def _cumsum_rows(tri_b, g):
    n = g.shape[1]
    hi = g.astype(BF16)
    r = g - hi.astype(F32)
    mid = r.astype(BF16)
    lo = (r - mid.astype(F32)).astype(BF16)
    s = _dot(tri_b, jnp.concatenate([hi, mid, lo], axis=1))
    return s[:, :n] + s[:, n:2 * n] + s[:, 2 * n:]


def _silu(x):
    return x * jax.nn.sigmoid(x)


def _layer_norm(h, g, b):
    mu = jnp.mean(h, axis=-1, keepdims=True)
    d = h - mu
    var = jnp.mean(d * d, axis=-1, keepdims=True)
    return d * lax.rsqrt(var + LN_EPS) * g + b


def _pad_rows(x, rows):
    if x.shape[0] == rows:
        return x
    return jnp.concatenate([x, jnp.zeros((rows - x.shape[0],) + x.shape[1:], x.dtype)], axis=0)


def _lane_pair(row, h0):
    lane = lax.broadcasted_iota(jnp.int32, (1, LANES), 1)
    return jnp.where(lane < SSM_P, jnp.broadcast_to(row[:, h0:h0 + 1], (1, LANES)),
                     jnp.broadcast_to(row[:, h0 + 1:h0 + 2], (1, LANES)))


def _cast_kernel(x_ref, o_ref):
    o_ref[...] = x_ref[...].astype(o_ref.dtype)


def _cast_bf16(w, rows=512):
    lead, r, c = w.shape
    return pl.pallas_call(
        _cast_kernel,
        grid=(lead, r // rows),
        in_specs=[pl.BlockSpec((None, rows, c), lambda l, i: (l, i, 0))],
        out_specs=pl.BlockSpec((None, rows, c), lambda l, i: (l, i, 0)),
        out_shape=jax.ShapeDtypeStruct(w.shape, BF16),
        compiler_params=_params("arbitrary", "arbitrary"),
        name="cast_bf16",
    )(w)


def _in_proj_kernel(x_ref, w_ref, o_ref, wb_ref):
    @pl.when(pl.program_id(1) == 0)
    def _():
        wb_ref[...] = w_ref[...].astype(BF16)

    o_ref[...] = _dot(x_ref[...], wb_ref[...])


def _in_proj(xb, w_in, layer, tn=1024):
    m, k = xb.shape
    return pl.pallas_call(
        _in_proj_kernel,
        grid=(N_MAIN // tn, m // TM),
        in_specs=[pl.BlockSpec((TM, k), lambda j, i: (i, 0)),
                  pl.BlockSpec((None, k, tn), lambda j, i: (layer, 0, j))],
        out_specs=pl.BlockSpec((TM, tn), lambda j, i: (i, j)),
        out_shape=jax.ShapeDtypeStruct((m, N_MAIN), F32),
        scratch_shapes=[pltpu.VMEM((k, tn), BF16)],
        compiler_params=_params("arbitrary", "arbitrary"),
        name="in_proj",
    )(xb, w_in)


def _gate_proj_kernel(x_ref, w0_ref, w1_ref, o_ref, wb_ref):
    tn = wb_ref.shape[1]

    @pl.when(pl.program_id(1) == 0)
    def _():
        w = jnp.concatenate([w0_ref[...], w1_ref[...]], axis=1)
        wb_ref[...] = w[:, GATE_SHIFT:GATE_SHIFT + tn].astype(BF16)

    o_ref[...] = jax.nn.sigmoid(_dot(x_ref[...], wb_ref[...]))


def _gate_proj(xb, w_in, layer, tn=512):
    m, k = xb.shape
    base = COL_DT // tn
    return pl.pallas_call(
        _gate_proj_kernel,
        grid=(2 * D_MODEL // tn, m // TM),
        in_specs=[pl.BlockSpec((TM, k), lambda j, i: (i, 0)),
                  pl.BlockSpec((None, k, tn), lambda j, i: (layer, 0, base + j)),
                  pl.BlockSpec((None, k, tn), lambda j, i: (layer, 0, base + j + 1))],
        out_specs=pl.BlockSpec((TM, tn), lambda j, i: (i, j)),
        out_shape=jax.ShapeDtypeStruct((m, 2 * D_MODEL), F32),
        scratch_shapes=[pltpu.VMEM((k, tn), BF16)],
        compiler_params=_params("arbitrary", "arbitrary"),
        name="gate_proj",
    )(xb, w_in, w_in)


def _dt_kernel(x_ref, w_ref, b_ref, o_ref, *, layer):
    raw = _dot(x_ref[...], w_ref[...].astype(BF16)) + b_ref[layer:layer + 1, :]
    o_ref[...] = jnp.maximum(raw, 0.0) + jnp.log1p(jnp.exp(-jnp.abs(raw)))


def _dt_proj(xb, w_in, dt_bias, layer):
    m, k = xb.shape
    return pl.pallas_call(
        functools.partial(_dt_kernel, layer=layer),
        grid=(m // TM,),
        in_specs=[pl.BlockSpec((TM, k), lambda i: (i, 0)),
                  pl.BlockSpec((None, k, LANES), lambda i: (layer, 0, COL_DT // LANES)),
                  pl.BlockSpec((DEPTH, LANES), lambda i: (0, 0))],
        out_specs=pl.BlockSpec((TM, LANES), lambda i: (i, 0)),
        out_shape=jax.ShapeDtypeStruct((m, LANES), F32),
        compiler_params=_params("arbitrary"),
        name="dt_proj",
    )(xb, w_in, dt_bias)


def _level_boundary(cum, h):
    c = cum.shape[0]
    if h >= SUBLANES:
        blk = cum.reshape(c // (2 * h), 2 * h, LANES)
        return jnp.broadcast_to(blk[:, h - 1:h, :], blk.shape).reshape(c, LANES)
    tiles = cum.reshape(c // SUBLANES, SUBLANES, LANES)
    sub = lax.broadcasted_iota(jnp.int32, tiles.shape, 1)
    out = None
    for start in range(0, SUBLANES, 2 * h):
        row = jnp.broadcast_to(tiles[:, start + h - 1:start + h, :], tiles.shape)
        out = row if out is None else jnp.where(sub >= start, row, out)
    return out.reshape(c, LANES)


def _gla_kernel(q_ref, f_ref, i_ref, g_ref, lbl_ref, nw_ref, s0_ref, tri_ref, msk_ref, buf_ref,
                o_ref, s_out_ref, st_ref, *, layer, heads, rows):
    del buf_ref
    c = pl.program_id(2)
    last = pl.num_programs(2) - 1

    @pl.when(c == 0)
    def _():
        for h in range(heads):
            st_ref[h] = s0_ref[0, h].T

    logits = lbl_ref[...]
    ex = jnp.exp(logits - jnp.max(logits, axis=0, keepdims=True))
    sm = ex / jnp.sum(ex, axis=0, keepdims=True)
    lb_all = jnp.zeros((1, sm.shape[1]), F32)
    for l in range(1, layer + 1):
        lb_all = lb_all + sm[l:l + 1]

    tri = tri_ref[...]
    valid = lax.broadcasted_iota(jnp.int32, (CHUNK, LANES), 0) < rows

    def finish(cols, a, v, o_inter):
        o = (_dot(a.astype(BF16), v) + o_inter)[:rows]
        o = o * lax.rsqrt(jnp.mean(o * o, axis=-1, keepdims=True) + RMS_EPS) * nw_ref[layer:layer + 1, cols]
        o_ref[:, cols] = (o * _silu(g_ref[:, cols])).astype(o_ref.dtype)

    pending = None
    for h in range(heads):
        cols = slice(h * HG_D, (h + 1) * HG_D)
        lb = lb_all[:, cols]
        q = _silu(_pad_rows(q_ref[:, cols], CHUNK))
        f = lb + (1.0 - lb) * jax.nn.sigmoid(_pad_rows(f_ref[:, cols], CHUNK))
        v32 = _pad_rows(i_ref[:, cols], CHUNK)
        v = v32.astype(BF16)
        k = 1.0 - f
        g = jnp.log(f)
        if rows < CHUNK:
            g = jnp.where(valid, g, 0.0)
        cum = _cumsum_rows(tri, g)

        st = st_ref[h]
        o_inter = _nt((q * jnp.exp(cum)).astype(BF16), st.astype(BF16))
        c_last = cum[CHUNK - 1:CHUNK, :]
        k_dec = k * jnp.exp(c_last - cum)
        st_ref[h] = st * jnp.exp(c_last) + _dot(v32.T.astype(BF16), k_dec.astype(BF16))

        a = _nt(q.astype(BF16), k.astype(BF16)) * msk_ref[N_LEVELS]
        for lev in range(N_LEVELS):
            z = jnp.exp(-jnp.abs(cum - _level_boundary(cum, 1 << lev)))
            a = a + _nt((q * z).astype(BF16), (k * z).astype(BF16)) * msk_ref[lev]
            if lev == 1 and pending is not None:
                finish(*pending)
        pending = (cols, a, v, o_inter)
    finish(*pending)

    @pl.when(c == last)
    def _():
        for h in range(heads):
            s_out_ref[0, h] = st_ref[h].T


def _gla(proj, buf, lb_logits, norm_w, s0, consts, *, layer, batch, seq, row0, heads_per_step=4):
    rows = min(seq, CHUNK)
    nc = seq // rows
    blk0 = row0 // rows
    hp = heads_per_step
    w = hp * HG_D

    def col(base):
        return lambda b, hb, c: (blk0 + b * nc + c, base // w + hb)

    par = lambda b, hb, c: (0, hb)
    state = lambda b, hb, c: (b, hb, 0, 0)
    out_rows = lambda b, hb, c: (blk0 + b * nc + c, hb)
    return pl.pallas_call(
        functools.partial(_gla_kernel, layer=layer, heads=hp, rows=rows),
        grid=(batch, HG_HEADS // hp, nc),
        in_specs=[pl.BlockSpec((rows, w), col(COL_Q)),
                  pl.BlockSpec((rows, w), col(COL_F)),
                  pl.BlockSpec((rows, w), col(COL_I)),
                  pl.BlockSpec((rows, w), col(COL_G)),
                  pl.BlockSpec((DEPTH, w), par),
                  pl.BlockSpec((DEPTH, w), par),
                  pl.BlockSpec((1, hp, HG_D, HG_D), state),
                  pl.BlockSpec((CHUNK, CHUNK), lambda b, hb, c: (0, 0)),
                  pl.BlockSpec((N_LEVELS + 1, CHUNK, CHUNK), lambda b, hb, c: (0, 0, 0)),
                  pl.BlockSpec((rows, w), out_rows)],
        out_specs=[pl.BlockSpec((rows, w), out_rows),
                   pl.BlockSpec((1, hp, HG_D, HG_D), state)],
        out_shape=[jax.ShapeDtypeStruct(buf.shape, buf.dtype),
                   jax.ShapeDtypeStruct((batch, HG_HEADS, HG_D, HG_D), F32)],
        scratch_shapes=[pltpu.VMEM((hp, HG_D, HG_D), F32)],
        input_output_aliases={9: 0},
        compiler_params=_params("arbitrary", "arbitrary", "arbitrary"),
        name="hgrn2",
    )(proj, proj, proj, proj, lb_logits, norm_w, s0, consts["tri"], consts["levels"], buf)


def _causal_conv4(raw, prev_ref, w, bias):
    n = raw.shape[0]
    ext = jnp.concatenate([prev_ref[...], raw], axis=0)
    out = bias + w[3:4] * raw + w[2:3] * ext[7:7 + n] + w[1:2] * ext[6:6 + n] + w[0:1] * ext[5:5 + n]
    prev_ref[...] = raw[n - SUBLANES:]
    return out


def _ssd_kernel(xs_ref, bc_ref, z_ref, dt_ref, cw_ref, cb_ref, st_ref, alog_ref, dskip_ref, nw_ref,
                h0_ref, tri_ref, buf_ref, y_ref, h_out_ref, st_out_ref, ht_ref, px_ref, pbc_ref,
                *, layer, rows):
    del buf_ref
    c = pl.program_id(1)
    last = pl.num_programs(1) - 1
    tail = SSM_CONV - 1

    @pl.when(c == 0)
    def _():
        ht_ref[...] = h0_ref[0].T
        buf = jnp.concatenate([jnp.zeros((SUBLANES - tail, SSM_XBC), F32), st_ref[0]], axis=0)
        px_ref[...] = buf[:, :SSM_INNER]
        pbc_ref[...] = buf[:, SSM_INNER:]

    raw_x, raw_bc = xs_ref[...], bc_ref[...]
    st_out_ref[0, :, :SSM_INNER] = raw_x[rows - tail:]
    st_out_ref[0, :, SSM_INNER:] = raw_bc[rows - tail:]
    cw = cw_ref[...]
    cb = cb_ref[layer:layer + 1, :]
    xs = _pad_rows(_silu(_causal_conv4(raw_x, px_ref, cw[:, :SSM_INNER], cb[:, :SSM_INNER])), CHUNK)
    bc = _pad_rows(_silu(_causal_conv4(raw_bc, pbc_ref, cw[:, SSM_INNER:], cb[:, SSM_INNER:])), CHUNK)

    dt = _pad_rows(dt_ref[...], CHUNK)
    if rows < CHUNK:
        dt = jnp.where(lax.broadcasted_iota(jnp.int32, dt.shape, 0) < rows, dt, 0.0)
    a = dt * (-jnp.exp(alog_ref[layer:layer + 1, :]))
    cum = _cumsum_rows(tri_ref[...], a)
    c_last = cum[CHUNK - 1:CHUNK, :]
    cum_t = cum.T
    dt_t = dt.T
    wgt_t = (jnp.exp(c_last - cum) * dt).T
    dec = jnp.exp(c_last)
    dskip = dskip_ref[layer:layer + 1, :]

    causal = (lax.broadcasted_iota(jnp.int32, (CHUNK, CHUNK), 0)
              >= lax.broadcasted_iota(jnp.int32, (CHUNK, CHUNK), 1))
    low = lax.broadcasted_iota(jnp.int32, (CHUNK, LANES), 1) < SSM_P
    z = z_ref[...]
    for g in range(SSM_GROUPS):
        bm = bc[:, g * SSM_N:(g + 1) * SSM_N]
        cm = bc[:, (SSM_GROUPS + g) * SSM_N:(SSM_GROUPS + g + 1) * SSM_N]
        cb_ts = _nt(cm.astype(BF16), bm.astype(BF16))
        bm_t = bm.T
        y_parts = []
        for pair in range(SSM_HPG // 2):
            h0 = g * SSM_HPG + 2 * pair
            cols = slice((h0 // 2) * LANES, (h0 // 2 + 1) * LANES)
            xp = xs[:, cols]
            ht = ht_ref[:, cols]
            lhs, rhs, upd_l, upd_r = [], [], [], []
            for h, mine in ((h0, low), (h0 + 1, jnp.logical_not(low))):
                col_b = jnp.broadcast_to(cum[:, h:h + 1], (CHUNK, CHUNK))
                lmat = jnp.exp(jnp.where(causal, col_b - cum_t[h:h + 1, :], NEG_BIG))
                lhs.append((cb_ts * lmat * dt_t[h:h + 1, :]).astype(BF16))
                lhs.append((cm * jnp.exp(col_b)).astype(BF16))
                xh = jnp.where(mine, xp, 0.0).astype(BF16)
                rhs.append(xh)
                rhs.append(jnp.where(mine, ht, 0.0).astype(BF16))
                upd_l.append((bm_t * wgt_t[h:h + 1, :]).astype(BF16))
                upd_r.append(xh)
            y_parts.append(_dot(jnp.concatenate(lhs, axis=1), jnp.concatenate(rhs, axis=0)))
            ht_ref[:, cols] = ht * _lane_pair(dec, h0) + _dot(jnp.concatenate(upd_l, axis=1),
                                                             jnp.concatenate(upd_r, axis=0))
            y_parts[-1] = y_parts[-1] + _lane_pair(dskip, h0) * xp
        gcols = slice(g * SSM_GW, (g + 1) * SSM_GW)
        y = jnp.concatenate(y_parts, axis=1)[:rows] * _silu(z[:, gcols])
        y = y * lax.rsqrt(jnp.mean(y * y, axis=-1, keepdims=True) + RMS_EPS) * nw_ref[layer:layer + 1, gcols]
        y_ref[:, gcols] = y.astype(y_ref.dtype)

    @pl.when(c == last)
    def _():
        h_out_ref[0] = ht_ref[...].T


def _ssd(proj, dt, buf, p, h0, conv0, consts, *, layer, batch, seq, row0):
    rows = min(seq, CHUNK)
    nc = seq // rows
    blk0 = row0 // rows
    n = SSM_N
    tail = SSM_CONV - 1
    h0 = h0.reshape(batch, SSM_INNER, n)
    row = lambda base, width: (lambda b, c: (blk0 + b * nc + c, base // width))
    par = lambda b, c: (0, 0)
    st = lambda b, c: (b, 0, 0)
    y, h_out, conv_out = pl.pallas_call(
        functools.partial(_ssd_kernel, layer=layer, rows=rows),
        grid=(batch, nc),
        in_specs=[pl.BlockSpec((rows, SSM_INNER), row(COL_XS, SSM_INNER)),
                  pl.BlockSpec((rows, SSM_BC), row(COL_BC, SSM_BC)),
                  pl.BlockSpec((rows, SSM_INNER), row(COL_Z, SSM_INNER)),
                  pl.BlockSpec((rows, LANES), row(0, LANES)),
                  pl.BlockSpec((None, SSM_CONV, SSM_XBC), lambda b, c: (layer, 0, 0)),
                  pl.BlockSpec((DEPTH, SSM_XBC), par),
                  pl.BlockSpec((1, tail, SSM_XBC), st),
                  pl.BlockSpec((DEPTH, LANES), par),
                  pl.BlockSpec((DEPTH, LANES), par),
                  pl.BlockSpec((DEPTH, SSM_INNER), par),
                  pl.BlockSpec((1, SSM_INNER, n), st),
                  pl.BlockSpec((CHUNK, CHUNK), par),
                  pl.BlockSpec((rows, SSM_INNER), row(0, SSM_INNER))],
        out_specs=[pl.BlockSpec((rows, SSM_INNER), row(0, SSM_INNER)),
                   pl.BlockSpec((1, SSM_INNER, n), st),
                   pl.BlockSpec((1, tail, SSM_XBC), st)],
        out_shape=[jax.ShapeDtypeStruct(buf.shape, buf.dtype),
                   jax.ShapeDtypeStruct((batch, SSM_INNER, n), F32),
                   jax.ShapeDtypeStruct((batch, tail, SSM_XBC), F32)],
        scratch_shapes=[pltpu.VMEM((n, SSM_INNER), F32),
                        pltpu.VMEM((SUBLANES, SSM_INNER), F32),
                        pltpu.VMEM((SUBLANES, SSM_BC), F32)],
        input_output_aliases={12: 0},
        compiler_params=_params("arbitrary", "arbitrary"),
        name="ssd",
    )(proj, proj, proj, dt, p["ssm_conv_w"], p["ssm_conv_b"], conv0, p["a_log"], p["d_skip"],
      p["ssm_norm_w"], h0, consts["tri"], buf)
    return y, h_out.reshape(batch, SSM_HEADS, SSM_P, n), conv_out


def _merge_kernel(oa_ref, ob_ref, wa_ref, wb_ref, ga_ref, gb_ref, o_ref, wab_ref, wbb_ref):
    @pl.when(pl.program_id(1) == 0)
    def _():
        wab_ref[...] = wa_ref[...].astype(BF16)
        wbb_ref[...] = wb_ref[...].astype(BF16)

    o_ref[...] = (ga_ref[...] * _dot(oa_ref[...], wab_ref[...])
                  + gb_ref[...] * _dot(ob_ref[...], wbb_ref[...])).astype(o_ref.dtype)


def _merge(o_a, o_b, w_a, w_b, gates, layer, tn=512):
    m, k = o_a.shape
    nj = D_MODEL // tn
    wspec = pl.BlockSpec((None, k, tn), lambda j, i: (layer, 0, j))
    return pl.pallas_call(
        _merge_kernel,
        grid=(nj, m // TM),
        in_specs=[pl.BlockSpec((TM, k), lambda j, i: (i, 0)),
                  pl.BlockSpec((TM, k), lambda j, i: (i, 0)),
                  wspec, wspec,
                  pl.BlockSpec((TM, tn), lambda j, i: (i, j)),
                  pl.BlockSpec((TM, tn), lambda j, i: (i, nj + j))],
        out_specs=pl.BlockSpec((TM, tn), lambda j, i: (i, j)),
        out_shape=jax.ShapeDtypeStruct((m, D_MODEL), BF16),
        scratch_shapes=[pltpu.VMEM((k, tn), BF16), pltpu.VMEM((k, tn), BF16)],
        compiler_params=_params("arbitrary", "arbitrary"),
        name="merge",
    )(o_a, o_b, w_a, w_b, gates, gates)


def _out_ln_kernel(mg_ref, w_ref, x_ref, g_ref, b_ref, o_ref, ob_ref, wb_ref, *, layer):
    @pl.when(pl.program_id(0) == 0)
    def _():
        wb_ref[...] = w_ref[...].astype(BF16)

    h = ALPHA * x_ref[...] + _dot(mg_ref[...], wb_ref[...])
    y = _layer_norm(h, g_ref[layer:layer + 1, :], b_ref[layer:layer + 1, :])
    o_ref[...] = y
    ob_ref[...] = y.astype(ob_ref.dtype)


def _out_ln(merged, w_out, x, g, b, layer):
    m, k = merged.shape
    row = lambda i: (i, 0)
    fix = lambda i: (0, 0)
    return pl.pallas_call(
        functools.partial(_out_ln_kernel, layer=layer),
        grid=(m // TM_LN,),
        in_specs=[pl.BlockSpec((TM_LN, k), row),
                  pl.BlockSpec((None, k, D_MODEL), lambda i: (layer, 0, 0), pipeline_mode=pl.Buffered(1)),
                  pl.BlockSpec((TM_LN, D_MODEL), row),
                  pl.BlockSpec((DEPTH, D_MODEL), fix),
                  pl.BlockSpec((DEPTH, D_MODEL), fix)],
        out_specs=[pl.BlockSpec((TM_LN, D_MODEL), row), pl.BlockSpec((TM_LN, D_MODEL), row)],
        out_shape=[jax.ShapeDtypeStruct((m, D_MODEL), F32), jax.ShapeDtypeStruct((m, D_MODEL), BF16)],
        scratch_shapes=[pltpu.VMEM((k, D_MODEL), BF16)],
        compiler_params=_params("arbitrary"),
        name="out_ln",
    )(merged, w_out, x, g, b)


def _gelu_gate(conv, v):
    return 0.5 * conv * (1.0 + lax.erf(conv * (1.0 / math.sqrt(2.0)))) * v


def _ffn_up_kernel(x_ref, wa_ref, wv_ref, cw_ref, cb_ref, st_ref, h_ref, st_s_ref, st_p_ref,
                   wab_ref, wvb_ref, prev_ref, meta_ref, *, layer, slot_rows, meta_slot, tiles_per_seq):
    i = pl.program_id(1)
    tn = wab_ref.shape[1]
    tail = FFN_CONV - 1

    @pl.when(i == 0)
    def _():
        wab_ref[...] = wa_ref[...].astype(BF16)
        wvb_ref[...] = wv_ref[...].astype(BF16)

    x = x_ref[...]
    a = _dot(x, wab_ref[...])
    v = _dot(x, wvb_ref[...])
    w = cw_ref[...]
    bias = cb_ref[layer:layer + 1, :]

    @pl.when(i == 0)
    def _():
        slots = TM // slot_rows
        n_s = st_ref.shape[0]
        a3 = a.reshape(slots, slot_rows, tn)
        prev = jnp.concatenate(
            [jnp.concatenate([jnp.zeros((n_s, SUBLANES - tail, tn), F32), st_ref[...]], axis=1),
             jnp.zeros((slots - n_s, SUBLANES, tn), F32)], axis=0)
        ext = jnp.concatenate([prev, a3], axis=1)
        n = slot_rows
        conv = bias + w[2:3] * a3 + w[1:2] * ext[:, 7:7 + n] + w[0:1] * ext[:, 6:6 + n]
        st_s_ref[...] = a3[:n_s, n - tail:]
        meta_ref[...] = a3[meta_slot, N_META - tail:N_META]
        h_ref[...] = _gelu_gate(conv, v.reshape(slots, slot_rows, tn)).reshape(TM, tn).astype(h_ref.dtype)

    @pl.when(i > 0)
    def _():
        @pl.when((i - 1) % tiles_per_seq == 0)
        def _():
            prev_ref[...] = jnp.concatenate([jnp.zeros((SUBLANES - tail, tn), F32), meta_ref[...]], axis=0)

        ext = jnp.concatenate([prev_ref[...], a], axis=0)
        conv = bias + w[2:3] * a + w[1:2] * ext[7:7 + TM] + w[0:1] * ext[6:6 + TM]
        prev_ref[...] = a[TM - SUBLANES:]
        st_p_ref[0] = a[TM - tail:]
        h_ref[...] = _gelu_gate(conv, v).astype(h_ref.dtype)


def _ffn_up(xb, w_up, conv_w, conv_b, st_sample, layer, *, batch, seq, slot_rows, meta_slot, tn=512):
    m, k = xb.shape
    nj = D_FF // tn
    tail = FFN_CONV - 1
    tps = seq // TM
    n_s = st_sample.shape[1]
    return pl.pallas_call(
        functools.partial(_ffn_up_kernel, layer=layer, slot_rows=slot_rows, meta_slot=meta_slot,
                          tiles_per_seq=tps),
        grid=(nj, m // TM),
        in_specs=[pl.BlockSpec((TM, k), lambda j, i: (i, 0)),
                  pl.BlockSpec((None, k, tn), lambda j, i: (layer, 0, j)),
                  pl.BlockSpec((None, k, tn), lambda j, i: (layer, 0, nj + j)),
                  pl.BlockSpec((None, FFN_CONV, tn), lambda j, i: (layer, 0, j)),
                  pl.BlockSpec((DEPTH, tn), lambda j, i: (0, j)),
                  pl.BlockSpec((None, n_s, tail, tn), lambda j, i: (layer, 0, 0, j))],
        out_specs=[pl.BlockSpec((TM, tn), lambda j, i: (i, j)),
                   pl.BlockSpec((n_s, tail, tn), lambda j, i: (0, 0, j)),
                   pl.BlockSpec((1, tail, tn), lambda j, i: (jnp.maximum(i - 1, 0) // tps, 0, j))],
        out_shape=[jax.ShapeDtypeStruct((m, D_FF), BF16),
                   jax.ShapeDtypeStruct((n_s, tail, D_FF), F32),
                   jax.ShapeDtypeStruct((batch, tail, D_FF), F32)],
        scratch_shapes=[pltpu.VMEM((k, tn), BF16), pltpu.VMEM((k, tn), BF16),
                        pltpu.VMEM((SUBLANES, tn), F32), pltpu.VMEM((tail, tn), F32)],
        compiler_params=_params("arbitrary", "arbitrary"),
        name="ffn_up",
    )(xb, w_up, w_up, conv_w, conv_b, st_sample)


def _ffn_down_kernel(h_ref, w_ref, x_ref, g_ref, b_ref, o_ref, ob_ref, *, layer):
    y = _layer_norm(ALPHA * x_ref[...] + _dot(h_ref[...], w_ref[...]),
                    g_ref[layer:layer + 1, :], b_ref[layer:layer + 1, :])
    o_ref[...] = y
    ob_ref[...] = y.astype(ob_ref.dtype)


def _ffn_down_final_kernel(h_ref, w_ref, x_ref, g_ref, b_ref, ys_ref, yp_ref, *, layer, prompt_tile0):
    i = pl.program_id(0)
    y = _layer_norm(ALPHA * x_ref[...] + _dot(h_ref[...], w_ref[...]),
                    g_ref[layer:layer + 1, :], b_ref[layer:layer + 1, :])

    @pl.when(i == 0)
    def _():
        ys_ref[...] = y

    @pl.when(i >= prompt_tile0)
    def _():
        yp_ref[...] = y


def _ffn_down(h, w_down_b, x, g, b, layer, *, final, n_sample, row_prompt):
    m, k = h.shape
    row = lambda i: (i, 0)
    fix = lambda i: (0, 0)
    in_specs = [pl.BlockSpec((TM_LN, k), row),
                pl.BlockSpec((None, k, D_MODEL), lambda i: (layer, 0, 0), pipeline_mode=pl.Buffered(1)),
                pl.BlockSpec((TM_LN, D_MODEL), row),
                pl.BlockSpec((DEPTH, D_MODEL), fix),
                pl.BlockSpec((DEPTH, D_MODEL), fix)]
    if not final:
        return pl.pallas_call(
            functools.partial(_ffn_down_kernel, layer=layer),
            grid=(m // TM_LN,),
            in_specs=in_specs,
            out_specs=[pl.BlockSpec((TM_LN, D_MODEL), row), pl.BlockSpec((TM_LN, D_MODEL), row)],
            out_shape=[jax.ShapeDtypeStruct((m, D_MODEL), F32), jax.ShapeDtypeStruct((m, D_MODEL), BF16)],
            compiler_params=_params("arbitrary"),
            name="ffn_down",
        )(h, w_down_b, x, g, b)
    assert n_sample == TM_LN and row_prompt % TM_LN == 0
    t0 = row_prompt // TM_LN
    return pl.pallas_call(
        functools.partial(_ffn_down_final_kernel, layer=layer, prompt_tile0=t0),
        grid=(m // TM_LN,),
        in_specs=in_specs,
        out_specs=[pl.BlockSpec((TM_LN, D_MODEL), fix),
                   pl.BlockSpec((TM_LN, D_MODEL), lambda i: (jnp.maximum(i - t0, 0), 0))],
        out_shape=[jax.ShapeDtypeStruct((n_sample, D_MODEL), F32),
                   jax.ShapeDtypeStruct((m - row_prompt, D_MODEL), F32)],
        compiler_params=_params("arbitrary"),
        name="ffn_down_final",
    )(h, w_down_b, x, g, b)


def _constants():
    t = jnp.arange(CHUNK)[:, None]
    s = jnp.arange(CHUNK)[None, :]
    levels = [((t > s) & (((t ^ s) >> lev) == 1)).astype(F32) for lev in range(N_LEVELS)]
    levels.append((t == s).astype(F32))
    return dict(tri=(t >= s).astype(BF16), levels=jnp.stack(levels))


def _head_lanes(v):
    return jnp.pad(v.astype(F32), ((0, 0), (0, LANES - SSM_HEADS)))


def kernel(x_prompt, x_sample, state_hgrn, state_ssm, state_ssm_conv, state_ffn_conv, meta_tokens, w_in,
           hgrn_lb_logits, hgrn_norm_w, w_proj_a, ssm_conv_w, ssm_conv_b, ssm_dt_bias, ssm_a_log, ssm_d,
           ssm_norm_w, w_proj_b, w_out, ln1_g, ln1_b, ffn_w_up, ffn_conv_w, ffn_conv_b, ffn_w_down, ln2_g, ln2_b):
    batch, seq, _ = x_prompt.shape
    dec_batch, dec_seq, _ = x_sample.shape
    n_sample = dec_batch * dec_seq
    row_meta = n_sample
    row_prompt = TM
    assert seq % TM == 0 and dec_seq >= N_META and row_meta + dec_seq <= TM and row_meta % dec_seq == 0
    m = row_prompt + batch * seq

    x = jnp.concatenate([x_sample.reshape(n_sample, D_MODEL).astype(F32), meta_tokens.astype(F32),
                         jnp.zeros((row_prompt - n_sample - N_META, D_MODEL), F32),
                         x_prompt.reshape(batch * seq, D_MODEL).astype(F32)], axis=0)
    xb = x.astype(BF16)
    consts = _constants()
    lb_logits = hgrn_lb_logits.astype(F32)
    ssm_p = dict(ssm_conv_w=ssm_conv_w, ssm_conv_b=ssm_conv_b, a_log=_head_lanes(ssm_a_log),
                 d_skip=_head_lanes(ssm_d), ssm_norm_w=ssm_norm_w)
    dt_bias = _head_lanes(ssm_dt_bias)
    w_down_b = _cast_bf16(ffn_w_down)
    groups = (dict(batch=1, seq=N_META, row0=row_meta), dict(batch=batch, seq=seq, row0=row_prompt),
              dict(batch=dec_batch, seq=dec_seq, row0=0))

    hg_p, ssm_pp, conv_p, ffn_p, hg_s, ssm_s, conv_s, ffn_s = ([] for _ in range(8))
    y_sample = y_prompt = None
    for l in range(DEPTH):
        proj = _in_proj(xb, w_in, l)
        gates = _gate_proj(xb, w_in, l)
        dt = _dt_proj(xb, w_in, dt_bias, l)

        o_a = jnp.zeros((m, HG_HEADS * HG_D), BF16)
        o_b = jnp.zeros((m, SSM_INNER), BF16)
        o_a, hg_m = _gla(proj, o_a, lb_logits, hgrn_norm_w, jnp.zeros((1, HG_HEADS, HG_D, HG_D), F32), consts,
                         layer=l, **groups[0])
        o_b, ssm_m, conv_m = _ssd(proj, dt, o_b, ssm_p, jnp.zeros((1, SSM_HEADS, SSM_P, SSM_N), F32),
                                  jnp.zeros((1, SSM_CONV - 1, SSM_XBC), F32), consts, layer=l, **groups[0])
        seed = lambda s: jnp.broadcast_to(s, (batch,) + s.shape[1:])
        o_a, s_hg = _gla(proj, o_a, lb_logits, hgrn_norm_w, seed(hg_m), consts, layer=l, **groups[1])
        o_b, s_ssm, s_conv = _ssd(proj, dt, o_b, ssm_p, seed(ssm_m), seed(conv_m), consts, layer=l, **groups[1])
        hg_p.append(s_hg)
        ssm_pp.append(s_ssm)
        conv_p.append(s_conv)
        o_a, s_hg = _gla(proj, o_a, lb_logits, hgrn_norm_w, state_hgrn[l], consts, layer=l, **groups[2])
        o_b, s_ssm, s_conv = _ssd(proj, dt, o_b, ssm_p, state_ssm[l], state_ssm_conv[l], consts, layer=l,
                                  **groups[2])
        hg_s.append(s_hg)
        ssm_s.append(s_ssm)
        conv_s.append(s_conv)

        merged = _merge(o_a, o_b, w_proj_a, w_proj_b, gates, l)
        x, xb = _out_ln(merged, w_out, x, ln1_g, ln1_b, l)
        hid, f_s, f_p = _ffn_up(xb, ffn_w_up, ffn_conv_w, ffn_conv_b, state_ffn_conv, l, batch=batch, seq=seq,
                                slot_rows=dec_seq, meta_slot=row_meta // dec_seq)
        ffn_s.append(f_s)
        ffn_p.append(f_p)
        if l < DEPTH - 1:
            x, xb = _ffn_down(hid, w_down_b, x, ln2_g, ln2_b, l, final=False, n_sample=n_sample,
                              row_prompt=row_prompt)
        else:
            y_sample, y_prompt = _ffn_down(hid, w_down_b, x, ln2_g, ln2_b, l, final=True, n_sample=n_sample,
                                           row_prompt=row_prompt)
    return (y_prompt.reshape(batch, seq, D_MODEL), y_sample.reshape(dec_batch, dec_seq, D_MODEL),
            jnp.stack(hg_p), jnp.stack(ssm_pp), jnp.stack(conv_p), jnp.stack(ffn_p),
            jnp.stack(hg_s), jnp.stack(ssm_s), jnp.stack(conv_s), jnp.stack(ffn_s))
```

```python
import functools
import math

import jax
import jax.numpy as jnp
from jax import lax
from jax.experimental import pallas as pl
from jax.experimental.pallas import tpu as pltpu

F32 = jnp.float32
BF16 = jnp.bfloat16

D_MODEL = 2048
DEPTH = 2
N_META = 16
HG_HEADS = 16
HG_D = 128
SSM_HEADS = 32
SSM_P = 64
SSM_GROUPS = 4
SSM_HPG = SSM_HEADS // SSM_GROUPS
SSM_N = 128
SSM_INNER = SSM_HEADS * SSM_P
SSM_GW = SSM_HPG * SSM_P
SSM_BC = 2 * SSM_GROUPS * SSM_N
SSM_CONV = 4
SSM_XBC = SSM_INNER + SSM_BC
D_FF = 5632
FFN_CONV = 3
ALPHA = (2.0 * DEPTH) ** 0.25
LN_EPS = 1e-5
RMS_EPS = 1e-6

LANES = 128
SUBLANES = 8
CHUNK = 128
N_LEVELS = int(math.log2(CHUNK))
TM = 512
TM_LN = 256
VMEM_LIMIT = 48 * 1024 * 1024
NEG_BIG = -1e30

COL_Q, COL_F, COL_I, COL_G, COL_Z = 0, 2048, 4096, 6144, 8192
COL_XS, COL_BC = 10240, 12288
N_MAIN = 13312
COL_DT = 13312
COL_GATES = COL_DT + SSM_HEADS
COL_GA, COL_GB = N_MAIN, N_MAIN + D_MODEL


def _params(*semantics):
    return pltpu.CompilerParams(dimension_semantics=semantics, vmem_limit_bytes=VMEM_LIMIT)


def _nt(a, b):
    return lax.dot_general(a, b, (((1,), (1,)), ((), ())), preferred_element_type=F32)


def _dot(a, b):
    return jnp.dot(a, b, preferred_element_type=F32)


def _cumsum_rows(tri_b, g):
    n = g.shape[1]
    hi = g.astype(BF16)
    lo = (g - hi.astype(F32)).astype(BF16)
    s = _dot(tri_b, jnp.concatenate([hi, lo], axis=1))
    return s[:, :n] + s[:, n:]


def _silu(x):
    return x * jax.nn.sigmoid(x)


def _layer_norm(h, g, b):
    mu = jnp.mean(h, axis=-1, keepdims=True)
    d = h - mu
    var = jnp.mean(d * d, axis=-1, keepdims=True)
    return d * lax.rsqrt(var + LN_EPS) * g + b


def _pad_rows(x, rows):
    if x.shape[0] == rows:
        return x
    return jnp.concatenate([x, jnp.zeros((rows - x.shape[0],) + x.shape[1:], x.dtype)], axis=0)


def _lane_pair(row, h0):
    lane = lax.broadcasted_iota(jnp.int32, (1, LANES), 1)
    return jnp.where(lane < SSM_P, jnp.broadcast_to(row[:, h0:h0 + 1], (1, LANES)),
                     jnp.broadcast_to(row[:, h0 + 1:h0 + 2], (1, LANES)))


def _cast_kernel(x_ref, o_ref):
    o_ref[...] = x_ref[...].astype(o_ref.dtype)


def _cast_bf16(w, rows=512):
    lead, r, c = w.shape
    return pl.pallas_call(
        _cast_kernel,
        grid=(lead, r // rows),
        in_specs=[pl.BlockSpec((None, rows, c), lambda l, i: (l, i, 0))],
        out_specs=pl.BlockSpec((None, rows, c), lambda l, i: (l, i, 0)),
        out_shape=jax.ShapeDtypeStruct(w.shape, BF16),
        compiler_params=_params("arbitrary", "arbitrary"),
        name="cast_bf16",
    )(w)


def _in_proj_kernel(x_ref, w_ref, o_ref, wb_ref):
    @pl.when(pl.program_id(1) == 0)
    def _():
        wb_ref[...] = w_ref[0].T.astype(BF16)

    o_ref[...] = _dot(x_ref[...], wb_ref[...])


def _in_proj(xb, w_in_t, layer, tn=1024):
    m, k = xb.shape
    n_main = N_MAIN // tn
    n_tiles = n_main + 2 * D_MODEL // tn
    assert COL_GATES % SUBLANES == 0 and tn % SUBLANES == 0

    def window(j, i):
        row8 = jnp.where(j < n_main, j * (tn // SUBLANES), COL_GATES // SUBLANES + (j - n_main) * (tn // SUBLANES))
        return (layer, SUBLANES * row8, 0)

    return pl.pallas_call(
        _in_proj_kernel,
        grid=(n_tiles, m // TM),
        in_specs=[pl.BlockSpec((TM, k), lambda j, i: (i, 0)),
                  pl.BlockSpec((pl.Element(1), pl.Element(tn), pl.Element(k)), window)],
        out_specs=pl.BlockSpec((TM, tn), lambda j, i: (i, j)),
        out_shape=jax.ShapeDtypeStruct((m, n_tiles * tn), F32),
        scratch_shapes=[pltpu.VMEM((k, tn), BF16)],
        compiler_params=_params("arbitrary", "arbitrary"),
        name="in_proj",
    )(xb, w_in_t)


def _dt_kernel(x_ref, w_ref, b_ref, o_ref, wb_ref, *, layer):
    @pl.when(pl.program_id(0) == 0)
    def _():
        wb_ref[...] = w_ref[...].T.astype(BF16)

    raw = _dot(x_ref[...], wb_ref[...]) + b_ref[layer:layer + 1, :]
    o_ref[...] = jnp.maximum(raw, 0.0) + jnp.log1p(jnp.exp(-jnp.abs(raw)))


def _dt_proj(xb, w_in_t, dt_bias, layer):
    m, k = xb.shape
    return pl.pallas_call(
        functools.partial(_dt_kernel, layer=layer),
        grid=(m // TM,),
        in_specs=[pl.BlockSpec((TM, k), lambda i: (i, 0)),
                  pl.BlockSpec((None, LANES, k), lambda i: (layer, COL_DT // LANES, 0)),
                  pl.BlockSpec((DEPTH, LANES), lambda i: (0, 0))],
        out_specs=pl.BlockSpec((TM, LANES), lambda i: (i, 0)),
        out_shape=jax.ShapeDtypeStruct((m, LANES), F32),
        scratch_shapes=[pltpu.VMEM((k, LANES), BF16)],
        compiler_params=_params("arbitrary"),
        name="dt_proj",
    )(xb, w_in_t, dt_bias)


def _level_boundary(cum, h):
    c = cum.shape[0]
    if h >= SUBLANES:
        blk = cum.reshape(c // (2 * h), 2 * h, LANES)
        return jnp.broadcast_to(blk[:, h - 1:h, :], blk.shape).reshape(c, LANES)
    tiles = cum.reshape(c // SUBLANES, SUBLANES, LANES)
    sub = lax.broadcasted_iota(jnp.int32, tiles.shape, 1)
    out = None
    for start in range(0, SUBLANES, 2 * h):
        row = jnp.broadcast_to(tiles[:, start + h - 1:start + h, :], tiles.shape)
        out = row if out is None else jnp.where(sub >= start, row, out)
    return out.reshape(c, LANES)


def _gla_kernel(q_ref, f_ref, i_ref, g_ref, lbl_ref, nw_ref, s0_ref, tri_ref, lvl_ref, buf_ref,
                o_ref, s_out_ref, st_ref, a_ref, *, layer, heads, rows):
    del buf_ref
    c = pl.program_id(2)
    last = pl.num_programs(2) - 1

    @pl.when(c == 0)
    def _():
        for h in range(heads):
            st_ref[h] = s0_ref[0, h].T

    logits = lbl_ref[...]
    ex = jnp.exp(logits - jnp.max(logits, axis=0, keepdims=True))
    sm = ex / jnp.sum(ex, axis=0, keepdims=True)
    lb_all = jnp.zeros((1, sm.shape[1]), F32)
    for l in range(1, layer + 1):
        lb_all = lb_all + sm[l:l + 1]

    tri = tri_ref[...]
    lvl = lvl_ref[...]
    valid = lax.broadcasted_iota(jnp.int32, (CHUNK, LANES), 0) < rows

    head_cols = [slice(h * HG_D, (h + 1) * HG_D) for h in range(heads)]
    qs, ks, vs, decays, cums, o_inters = [], [], [], [], [], []
    for h, cols in enumerate(head_cols):
        lb = lb_all[:, cols]
        q = _silu(_pad_rows(q_ref[:, cols], CHUNK))
        f = lb + (1.0 - lb) * jax.nn.sigmoid(_pad_rows(f_ref[:, cols], CHUNK))
        decay = f if rows == CHUNK else jnp.where(valid, f, 1.0)
        qs.append(q)
        ks.append(1.0 - f)
        decays.append(decay)
        cums.append(_cumsum_rows(tri, jnp.log2(decay)))

    for h, cols in enumerate(head_cols):
        q, k, cum = qs[h], ks[h], cums[h]
        v32 = _pad_rows(i_ref[:, cols], CHUNK)
        vs.append(v32.astype(BF16))
        st = st_ref[h]
        o_inters.append(_nt((q * jnp.exp2(cum)).astype(BF16), st.astype(BF16)))
        c_last = cum[CHUNK - 1:CHUNK, :]
        k_dec = k * jnp.exp2(c_last - cum)
        st_ref[h] = st * jnp.exp2(c_last) + _dot(v32.T.astype(BF16), k_dec.astype(BF16))

    kbs = [k.astype(BF16) for k in ks]
    for h in range(heads):
        a_ref[h] = jnp.where(lvl == N_LEVELS, _nt(qs[h].astype(BF16), kbs[h]), 0.0)
    for h in range(heads):
        pltpu.store(a_ref.at[h], _nt((qs[h] * decays[h]).astype(BF16), kbs[h]), mask=lvl == 0)
    for lev in range(1, N_LEVELS):
        half = 1 << lev
        for h in range(heads):
            q, k, cum = qs[h], ks[h], cums[h]
            if half < SUBLANES:
                z = jnp.exp2(-jnp.abs(cum - _level_boundary(cum, half)))
                pltpu.store(a_ref.at[h], _nt((q * z).astype(BF16), (k * z).astype(BF16)), mask=lvl == lev)
            else:
                q_parts, k_parts, uppers = [], [], []
                for start in range(0, CHUNK, 2 * half):
                    lo, up = slice(start, start + half), slice(start + half, start + 2 * half)
                    mid = cum[start + half - 1:start + half, :]
                    q_parts.append(q[up] * jnp.exp2(cum[up] - mid))
                    k_parts += [k[lo] * jnp.exp2(mid - cum[lo]), k[up]]
                    uppers.append(up)
                p = _nt(jnp.concatenate(q_parts, axis=0).astype(BF16),
                        jnp.concatenate(k_parts, axis=0).astype(BF16))
                for n, up in enumerate(uppers):
                    pltpu.store(a_ref.at[h, up, :], p[n * half:(n + 1) * half, :], mask=lvl[up, :] == lev)

    for h, cols in enumerate(head_cols):
        o = (_dot(a_ref[h].astype(BF16), vs[h]) + o_inters[h])[:rows]
        o = o * lax.rsqrt(jnp.mean(o * o, axis=-1, keepdims=True) + RMS_EPS) * nw_ref[layer:layer + 1, cols]
        o_ref[:, cols] = (o * _silu(g_ref[:, cols])).astype(o_ref.dtype)

    @pl.when(c == last)
    def _():
        for h in range(heads):
            s_out_ref[0, h] = st_ref[h].T


def _gla(proj, buf, lb_logits, norm_w, s0, consts, *, layer, batch, seq, row0, heads_per_step=4):
    rows = min(seq, CHUNK)
    nc = seq // rows
    blk0 = row0 // rows
    hp = heads_per_step
    w = hp * HG_D

    def col(base):
        return lambda b, hb, c: (blk0 + b * nc + c, base // w + hb)

    par = lambda b, hb, c: (0, hb)
    state = lambda b, hb, c: (b, hb, 0, 0)
    out_rows = lambda b, hb, c: (blk0 + b * nc + c, hb)
    return pl.pallas_call(
        functools.partial(_gla_kernel, layer=layer, heads=hp, rows=rows),
        grid=(batch, HG_HEADS // hp, nc),
        in_specs=[pl.BlockSpec((rows, w), col(COL_Q)),
                  pl.BlockSpec((rows, w), col(COL_F)),
                  pl.BlockSpec((rows, w), col(COL_I)),
                  pl.BlockSpec((rows, w), col(COL_G)),
                  pl.BlockSpec((DEPTH, w), par),
                  pl.BlockSpec((DEPTH, w), par),
                  pl.BlockSpec((1, hp, HG_D, HG_D), state),
                  pl.BlockSpec((CHUNK, CHUNK), lambda b, hb, c: (0, 0)),
                  pl.BlockSpec((CHUNK, CHUNK), lambda b, hb, c: (0, 0)),
                  pl.BlockSpec((rows, w), out_rows)],
        out_specs=[pl.BlockSpec((rows, w), out_rows),
                   pl.BlockSpec((1, hp, HG_D, HG_D), state)],
        out_shape=[jax.ShapeDtypeStruct(buf.shape, buf.dtype),
                   jax.ShapeDtypeStruct((batch, HG_HEADS, HG_D, HG_D), F32)],
        scratch_shapes=[pltpu.VMEM((hp, HG_D, HG_D), F32), pltpu.VMEM((hp, CHUNK, CHUNK), F32)],
        input_output_aliases={9: 0},
        compiler_params=_params("arbitrary", "arbitrary", "arbitrary"),
        name="hgrn2",
    )(proj, proj, proj, proj, lb_logits, norm_w, s0, consts["tri"], consts["level"], buf)


def _causal_conv4(raw, prev_ref, w, bias):
    n = raw.shape[0]
    ext = jnp.concatenate([prev_ref[...], raw], axis=0)
    out = bias + w[3:4] * raw + w[2:3] * ext[7:7 + n] + w[1:2] * ext[6:6 + n] + w[0:1] * ext[5:5 + n]
    prev_ref[...] = raw[n - SUBLANES:]
    return out


def _ssd_kernel(xs_ref, bc_ref, z_ref, dt_ref, cw_ref, cb_ref, st_ref, alog_ref, dskip_ref, nw_ref,
                h0_ref, tri_ref, buf_ref, y_ref, h_out_ref, st_out_ref, ht_ref, px_ref, pbc_ref,
                *, layer, rows):
    del buf_ref
    c = pl.program_id(1)
    last = pl.num_programs(1) - 1
    tail = SSM_CONV - 1

    @pl.when(c == 0)
    def _():
        ht_ref[...] = h0_ref[0].T
        buf = jnp.concatenate([jnp.zeros((SUBLANES - tail, SSM_XBC), F32), st_ref[0]], axis=0)
        px_ref[...] = buf[:, :SSM_INNER]
        pbc_ref[...] = buf[:, SSM_INNER:]

    raw_x, raw_bc = xs_ref[...], bc_ref[...]
    st_out_ref[0, :, :SSM_INNER] = raw_x[rows - tail:]
    st_out_ref[0, :, SSM_INNER:] = raw_bc[rows - tail:]
    cw = cw_ref[...]
    cb = cb_ref[layer:layer + 1, :]
    xs = _pad_rows(_silu(_causal_conv4(raw_x, px_ref, cw[:, :SSM_INNER], cb[:, :SSM_INNER])), CHUNK)
    bc = _pad_rows(_silu(_causal_conv4(raw_bc, pbc_ref, cw[:, SSM_INNER:], cb[:, SSM_INNER:])), CHUNK)

    dt = _pad_rows(dt_ref[...], CHUNK)
    if rows < CHUNK:
        dt = jnp.where(lax.broadcasted_iota(jnp.int32, dt.shape, 0) < rows, dt, 0.0)
    a = dt * (-jnp.exp(alog_ref[layer:layer + 1, :]))
    cum = _cumsum_rows(tri_ref[...], a)
    c_last = cum[CHUNK - 1:CHUNK, :]
    cum_t = cum.T
    dt_t = dt.T
    wgt_t = (jnp.exp(c_last - cum) * dt).T
    dec = jnp.exp(c_last)
    dskip = dskip_ref[layer:layer + 1, :]

    causal = (lax.broadcasted_iota(jnp.int32, (CHUNK, CHUNK), 0)
              >= lax.broadcasted_iota(jnp.int32, (CHUNK, CHUNK), 1))
    low = lax.broadcasted_iota(jnp.int32, (CHUNK, LANES), 1) < SSM_P
    z = z_ref[...]
    for g in range(SSM_GROUPS):
        bm = bc[:, g * SSM_N:(g + 1) * SSM_N]
        cm = bc[:, (SSM_GROUPS + g) * SSM_N:(SSM_GROUPS + g + 1) * SSM_N]
        cb_ts = _nt(cm.astype(BF16), bm.astype(BF16))
        bm_t = bm.T
        y_parts = []
        for pair in range(SSM_HPG // 2):
            h0 = g * SSM_HPG + 2 * pair
            cols = slice((h0 // 2) * LANES, (h0 // 2 + 1) * LANES)
            xp = xs[:, cols]
            ht = ht_ref[:, cols]
            lhs, rhs, upd_l, upd_r = [], [], [], []
            for h, mine in ((h0, low), (h0 + 1, jnp.logical_not(low))):
                col_b = jnp.broadcast_to(cum[:, h:h + 1], (CHUNK, CHUNK))
                lmat = jnp.exp(jnp.where(causal, col_b - cum_t[h:h + 1, :], NEG_BIG))
                lhs.append((cb_ts * lmat * dt_t[h:h + 1, :]).astype(BF16))
                lhs.append((cm * jnp.exp(col_b)).astype(BF16))
                xh = jnp.where(mine, xp, 0.0).astype(BF16)
                rhs.append(xh)
                rhs.append(jnp.where(mine, ht, 0.0).astype(BF16))
                upd_l.append((bm_t * wgt_t[h:h + 1, :]).astype(BF16))
                upd_r.append(xh)
            y_parts.append(_dot(jnp.concatenate(lhs, axis=1), jnp.concatenate(rhs, axis=0)))
            ht_ref[:, cols] = ht * _lane_pair(dec, h0) + _dot(jnp.concatenate(upd_l, axis=1),
                                                             jnp.concatenate(upd_r, axis=0))
            y_parts[-1] = y_parts[-1] + _lane_pair(dskip, h0) * xp
        gcols = slice(g * SSM_GW, (g + 1) * SSM_GW)
        y = jnp.concatenate(y_parts, axis=1)[:rows] * _silu(z[:, gcols])
        y = y * lax.rsqrt(jnp.mean(y * y, axis=-1, keepdims=True) + RMS_EPS) * nw_ref[layer:layer + 1, gcols]
        y_ref[:, gcols] = y.astype(y_ref.dtype)

    @pl.when(c == last)
    def _():
        h_out_ref[0] = ht_ref[...].T


def _ssd(proj, dt, buf, p, h0, conv0, consts, *, layer, batch, seq, row0):
    rows = min(seq, CHUNK)
    nc = seq // rows
    blk0 = row0 // rows
    n = SSM_N
    tail = SSM_CONV - 1
    h0 = h0.reshape(batch, SSM_INNER, n)
    row = lambda base, width: (lambda b, c: (blk0 + b * nc + c, base // width))
    par = lambda b, c: (0, 0)
    st = lambda b, c: (b, 0, 0)
    y, h_out, conv_out = pl.pallas_call(
        functools.partial(_ssd_kernel, layer=layer, rows=rows),
        grid=(batch, nc),
        in_specs=[pl.BlockSpec((rows, SSM_INNER), row(COL_XS, SSM_INNER)),
                  pl.BlockSpec((rows, SSM_BC), row(COL_BC, SSM_BC)),
                  pl.BlockSpec((rows, SSM_INNER), row(COL_Z, SSM_INNER)),
                  pl.BlockSpec((rows, LANES), row(0, LANES)),
                  pl.BlockSpec((None, SSM_CONV, SSM_XBC), lambda b, c: (layer, 0, 0)),
                  pl.BlockSpec((DEPTH, SSM_XBC), par),
                  pl.BlockSpec((1, tail, SSM_XBC), st),
                  pl.BlockSpec((DEPTH, LANES), par),
                  pl.BlockSpec((DEPTH, LANES), par),
                  pl.BlockSpec((DEPTH, SSM_INNER), par),
                  pl.BlockSpec((1, SSM_INNER, n), st),
                  pl.BlockSpec((CHUNK, CHUNK), par),
                  pl.BlockSpec((rows, SSM_INNER), row(0, SSM_INNER))],
        out_specs=[pl.BlockSpec((rows, SSM_INNER), row(0, SSM_INNER)),
                   pl.BlockSpec((1, SSM_INNER, n), st),
                   pl.BlockSpec((1, tail, SSM_XBC), st)],
        out_shape=[jax.ShapeDtypeStruct(buf.shape, buf.dtype),
                   jax.ShapeDtypeStruct((batch, SSM_INNER, n), F32),
                   jax.ShapeDtypeStruct((batch, tail, SSM_XBC), F32)],
        scratch_shapes=[pltpu.VMEM((n, SSM_INNER), F32),
                        pltpu.VMEM((SUBLANES, SSM_INNER), F32),
                        pltpu.VMEM((SUBLANES, SSM_BC), F32)],
        input_output_aliases={12: 0},
        compiler_params=_params("arbitrary", "arbitrary"),
        name="ssd",
    )(proj, proj, proj, dt, p["ssm_conv_w"], p["ssm_conv_b"], conv0, p["a_log"], p["d_skip"],
      p["ssm_norm_w"], h0, consts["tri"], buf)
    return y, h_out.reshape(batch, SSM_HEADS, SSM_P, n), conv_out


def _merge_kernel(oa_ref, ob_ref, wa_ref, wb_ref, ga_ref, gb_ref, o_ref, wab_ref, wbb_ref):
    @pl.when(pl.program_id(1) == 0)
    def _():
        wab_ref[...] = wa_ref[...].astype(BF16)
        wbb_ref[...] = wb_ref[...].astype(BF16)

    o_ref[...] = (jax.nn.sigmoid(ga_ref[...]) * _dot(oa_ref[...], wab_ref[...])
                  + jax.nn.sigmoid(gb_ref[...]) * _dot(ob_ref[...], wbb_ref[...])).astype(o_ref.dtype)


def _merge(o_a, o_b, w_a, w_b, proj, layer, tn=1024):
    m, k = o_a.shape
    wspec = pl.BlockSpec((None, k, tn), lambda j, i: (layer, 0, j), pipeline_mode=pl.Buffered(1))
    return pl.pallas_call(
        _merge_kernel,
        grid=(D_MODEL // tn, m // TM),
        in_specs=[pl.BlockSpec((TM, k), lambda j, i: (i, 0)),
                  pl.BlockSpec((TM, k), lambda j, i: (i, 0)),
                  wspec, wspec,
                  pl.BlockSpec((TM, tn), lambda j, i: (i, COL_GA // tn + j)),
                  pl.BlockSpec((TM, tn), lambda j, i: (i, COL_GB // tn + j))],
        out_specs=pl.BlockSpec((TM, tn), lambda j, i: (i, j)),
        out_shape=jax.ShapeDtypeStruct((m, D_MODEL), BF16),
        scratch_shapes=[pltpu.VMEM((k, tn), BF16), pltpu.VMEM((k, tn), BF16)],
        compiler_params=_params("arbitrary", "arbitrary"),
        name="merge",
    )(o_a, o_b, w_a, w_b, proj, proj)


def _out_ln_kernel(mg_ref, w_ref, x_ref, g_ref, b_ref, o_ref, ob_ref, wb_ref, *, layer):
    @pl.when(pl.program_id(0) == 0)
    def _():
        wb_ref[...] = w_ref[...].astype(BF16)

    h = ALPHA * x_ref[...] + _dot(mg_ref[...], wb_ref[...])
    y = _layer_norm(h, g_ref[layer:layer + 1, :], b_ref[layer:layer + 1, :])
    o_ref[...] = y
    ob_ref[...] = y.astype(ob_ref.dtype)


def _out_ln(merged, w_out, x, g, b, layer):
    m, k = merged.shape
    row = lambda i: (i, 0)
    fix = lambda i: (0, 0)
    return pl.pallas_call(
        functools.partial(_out_ln_kernel, layer=layer),
        grid=(m // TM_LN,),
        in_specs=[pl.BlockSpec((TM_LN, k), row),
                  pl.BlockSpec((None, k, D_MODEL), lambda i: (layer, 0, 0), pipeline_mode=pl.Buffered(1)),
                  pl.BlockSpec((TM_LN, D_MODEL), row),
                  pl.BlockSpec((DEPTH, D_MODEL), fix),
                  pl.BlockSpec((DEPTH, D_MODEL), fix)],
        out_specs=[pl.BlockSpec((TM_LN, D_MODEL), row), pl.BlockSpec((TM_LN, D_MODEL), row)],
        out_shape=[jax.ShapeDtypeStruct((m, D_MODEL), F32), jax.ShapeDtypeStruct((m, D_MODEL), BF16)],
        scratch_shapes=[pltpu.VMEM((k, D_MODEL), BF16)],
        compiler_params=_params("arbitrary"),
        name="out_ln",
    )(merged, w_out, x, g, b)


def _gelu(x):
    return 0.5 * x * (1.0 + lax.erf(x * (1.0 / math.sqrt(2.0))))


def _ffn_up_kernel(x_ref, wa_ref, wv_ref, cw_ref, cb_ref, st_ref, h_ref, st_s_ref, st_p_ref,
                   wab_ref, wvb_ref, prev_ref, meta_ref, *, layer, slot_rows, meta_slot, tiles_per_seq):
    i = pl.program_id(1)
    tn = wab_ref.shape[1]
    tail = FFN_CONV - 1

    @pl.when(i == 0)
    def _():
        wab_ref[...] = wa_ref[...].astype(BF16)
        wvb_ref[...] = wv_ref[...].astype(BF16)

    x = x_ref[...]
    a = _dot(x, wab_ref[...])
    w = cw_ref[...]
    bias = cb_ref[layer:layer + 1, :]

    def gated(conv):
        act = _gelu(conv).reshape(TM, tn)
        return (act * _dot(x, wvb_ref[...])).astype(h_ref.dtype)

    @pl.when(i == 0)
    def _():
        slots = TM // slot_rows
        n_s = st_ref.shape[0]
        a3 = a.reshape(slots, slot_rows, tn)
        prev = jnp.concatenate(
            [jnp.concatenate([jnp.zeros((n_s, SUBLANES - tail, tn), F32), st_ref[...]], axis=1),
             jnp.zeros((slots - n_s, SUBLANES, tn), F32)], axis=0)
        ext = jnp.concatenate([prev, a3], axis=1)
        n = slot_rows
        conv = bias + w[2:3] * a3 + w[1:2] * ext[:, 7:7 + n] + w[0:1] * ext[:, 6:6 + n]
        st_s_ref[...] = a3[:n_s, n - tail:]
        meta_ref[...] = a3[meta_slot, N_META - tail:N_META]
        h_ref[...] = gated(conv)

    @pl.when(i > 0)
    def _():
        @pl.when((i - 1) % tiles_per_seq == 0)
        def _():
            prev_ref[...] = jnp.concatenate([jnp.zeros((SUBLANES - tail, tn), F32), meta_ref[...]], axis=0)

        ext = jnp.concatenate([prev_ref[...], a], axis=0)
        conv = bias + w[2:3] * a + w[1:2] * ext[7:7 + TM] + w[0:1] * ext[6:6 + TM]
        prev_ref[...] = a[TM - SUBLANES:]
        st_p_ref[0] = a[TM - tail:]
        h_ref[...] = gated(conv)


def _ffn_up(xb, w_up, conv_w, conv_b, st_sample, layer, *, batch, seq, slot_rows, meta_slot, tn=512):
    m, k = xb.shape
    nj = D_FF // tn
    tail = FFN_CONV - 1
    tps = seq // TM
    n_s = st_sample.shape[1]
    return pl.pallas_call(
        functools.partial(_ffn_up_kernel, layer=layer, slot_rows=slot_rows, meta_slot=meta_slot,
                          tiles_per_seq=tps),
        grid=(nj, m // TM),
        in_specs=[pl.BlockSpec((TM, k), lambda j, i: (i, 0)),
                  pl.BlockSpec((None, k, tn), lambda j, i: (layer, 0, j)),
                  pl.BlockSpec((None, k, tn), lambda j, i: (layer, 0, nj + j)),
                  pl.BlockSpec((None, FFN_CONV, tn), lambda j, i: (layer, 0, j)),
                  pl.BlockSpec((DEPTH, tn), lambda j, i: (0, j)),
                  pl.BlockSpec((None, n_s, tail, tn), lambda j, i: (layer, 0, 0, j))],
        out_specs=[pl.BlockSpec((TM, tn), lambda j, i: (i, j)),
                   pl.BlockSpec((n_s, tail, tn), lambda j, i: (0, 0, j)),
                   pl.BlockSpec((1, tail, tn), lambda j, i: (jnp.maximum(i - 1, 0) // tps, 0, j))],
        out_shape=[jax.ShapeDtypeStruct((m, D_FF), BF16),
                   jax.ShapeDtypeStruct((n_s, tail, D_FF), F32),
                   jax.ShapeDtypeStruct((batch, tail, D_FF), F32)],
        scratch_shapes=[pltpu.VMEM((k, tn), BF16), pltpu.VMEM((k, tn), BF16),
                        pltpu.VMEM((SUBLANES, tn), F32), pltpu.VMEM((tail, tn), F32)],
        compiler_params=_params("arbitrary", "arbitrary"),
        name="ffn_up",
    )(xb, w_up, w_up, conv_w, conv_b, st_sample)


def _ffn_down_kernel(h_ref, w_ref, x_ref, g_ref, b_ref, o_ref, ob_ref, *, layer):
    y = _layer_norm(ALPHA * x_ref[...] + _dot(h_ref[...], w_ref[...]),
                    g_ref[layer:layer + 1, :], b_ref[layer:layer + 1, :])
    o_ref[...] = y
    ob_ref[...] = y.astype(ob_ref.dtype)


def _ffn_down_final_kernel(h_ref, w_ref, x_ref, g_ref, b_ref, ys_ref, yp_ref, *, layer, prompt_tile0):
    i = pl.program_id(0)
    y = _layer_norm(ALPHA * x_ref[...] + _dot(h_ref[...], w_ref[...]),
                    g_ref[layer:layer + 1, :], b_ref[layer:layer + 1, :])

    @pl.when(i == 0)
    def _():
        ys_ref[...] = y

    @pl.when(i >= prompt_tile0)
    def _():
        yp_ref[...] = y


def _ffn_down(h, w_down_b, x, g, b, layer, *, final, n_sample, row_prompt):
    m, k = h.shape
    row = lambda i: (i, 0)
    fix = lambda i: (0, 0)
    in_specs = [pl.BlockSpec((TM_LN, k), row),
                pl.BlockSpec((None, k, D_MODEL), lambda i: (layer, 0, 0), pipeline_mode=pl.Buffered(1)),
                pl.BlockSpec((TM_LN, D_MODEL), row),
                pl.BlockSpec((DEPTH, D_MODEL), fix),
                pl.BlockSpec((DEPTH, D_MODEL), fix)]
    if not final:
        return pl.pallas_call(
            functools.partial(_ffn_down_kernel, layer=layer),
            grid=(m // TM_LN,),
            in_specs=in_specs,
            out_specs=[pl.BlockSpec((TM_LN, D_MODEL), row), pl.BlockSpec((TM_LN, D_MODEL), row)],
            out_shape=[jax.ShapeDtypeStruct((m, D_MODEL), F32), jax.ShapeDtypeStruct((m, D_MODEL), BF16)],
            compiler_params=_params("arbitrary"),
            name="ffn_down",
        )(h, w_down_b, x, g, b)
    assert n_sample == TM_LN and row_prompt % TM_LN == 0
    t0 = row_prompt // TM_LN
    return pl.pallas_call(
        functools.partial(_ffn_down_final_kernel, layer=layer, prompt_tile0=t0),
        grid=(m // TM_LN,),
        in_specs=in_specs,
        out_specs=[pl.BlockSpec((TM_LN, D_MODEL), fix),
                   pl.BlockSpec((TM_LN, D_MODEL), lambda i: (jnp.maximum(i - t0, 0), 0))],
        out_shape=[jax.ShapeDtypeStruct((n_sample, D_MODEL), F32),
                   jax.ShapeDtypeStruct((m - row_prompt, D_MODEL), F32)],
        compiler_params=_params("arbitrary"),
        name="ffn_down_final",
    )(h, w_down_b, x, g, b)


def _constants():
    t = jnp.arange(CHUNK)[:, None]
    s = jnp.arange(CHUNK)[None, :]
    level = jnp.full((CHUNK, CHUNK), -1, jnp.int32)
    for lev in range(N_LEVELS):
        level = jnp.where((t > s) & (((t ^ s) >> lev) == 1), lev, level)
    level = jnp.where(t == s, N_LEVELS, level)
    return dict(tri=(t >= s).astype(BF16), level=level)


def _head_lanes(v):
    return jnp.pad(v.astype(F32), ((0, 0), (0, LANES - SSM_HEADS)))


def kernel(x_prompt, x_sample, state_hgrn, state_ssm, state_ssm_conv, state_ffn_conv, meta_tokens, w_in,
           hgrn_lb_logits, hgrn_norm_w, w_proj_a, ssm_conv_w, ssm_conv_b, ssm_dt_bias, ssm_a_log, ssm_d,
           ssm_norm_w, w_proj_b, w_out, ln1_g, ln1_b, ffn_w_up, ffn_conv_w, ffn_conv_b, ffn_w_down, ln2_g, ln2_b):
    batch, seq, _ = x_prompt.shape
    dec_batch, dec_seq, _ = x_sample.shape
    n_sample = dec_batch * dec_seq
    row_meta = n_sample
    row_prompt = TM
    assert seq % TM == 0 and dec_seq >= N_META and row_meta + dec_seq <= TM and row_meta % dec_seq == 0
    m = row_prompt + batch * seq

    x = jnp.concatenate([x_sample.reshape(n_sample, D_MODEL).astype(F32), meta_tokens.astype(F32),
                         jnp.zeros((row_prompt - n_sample - N_META, D_MODEL), F32),
                         x_prompt.reshape(batch * seq, D_MODEL).astype(F32)], axis=0)
    xb = x.astype(BF16)
    consts = _constants()
    lb_logits = hgrn_lb_logits.astype(F32)
    ssm_p = dict(ssm_conv_w=ssm_conv_w, ssm_conv_b=ssm_conv_b, a_log=_head_lanes(ssm_a_log),
                 d_skip=_head_lanes(ssm_d), ssm_norm_w=ssm_norm_w)
    dt_bias = _head_lanes(ssm_dt_bias)
    w_down_b = _cast_bf16(ffn_w_down)
    groups = (dict(batch=1, seq=N_META, row0=row_meta), dict(batch=batch, seq=seq, row0=row_prompt),
              dict(batch=dec_batch, seq=dec_seq, row0=0))

    w_in_t = jnp.swapaxes(w_in, 1, 2)
    o_a = jnp.zeros((m, HG_HEADS * HG_D), BF16)
    o_b = jnp.zeros((m, SSM_INNER), BF16)

    hg_p, ssm_pp, conv_p, ffn_p, hg_s, ssm_s, conv_s, ffn_s = ([] for _ in range(8))
    y_sample = y_prompt = None
    for l in range(DEPTH):
        proj = _in_proj(xb, w_in_t, l)
        dt = _dt_proj(xb, w_in_t, dt_bias, l)

        o_a, hg_m = _gla(proj, o_a, lb_logits, hgrn_norm_w, jnp.zeros((1, HG_HEADS, HG_D, HG_D), F32), consts,
                         layer=l, **groups[0])
        o_b, ssm_m, conv_m = _ssd(proj, dt, o_b, ssm_p, jnp.zeros((1, SSM_HEADS, SSM_P, SSM_N), F32),
                                  jnp.zeros((1, SSM_CONV - 1, SSM_XBC), F32), consts, layer=l, **groups[0])
        seed = lambda s: jnp.broadcast_to(s, (batch,) + s.shape[1:])
        o_a, s_hg = _gla(proj, o_a, lb_logits, hgrn_norm_w, seed(hg_m), consts, layer=l, **groups[1])
        o_b, s_ssm, s_conv = _ssd(proj, dt, o_b, ssm_p, seed(ssm_m), seed(conv_m), consts, layer=l, **groups[1])
        hg_p.append(s_hg)
        ssm_pp.append(s_ssm)
        conv_p.append(s_conv)
        o_a, s_hg = _gla(proj, o_a, lb_logits, hgrn_norm_w, state_hgrn[l], consts, layer=l, **groups[2])
        o_b, s_ssm, s_conv = _ssd(proj, dt, o_b, ssm_p, state_ssm[l], state_ssm_conv[l], consts, layer=l,
                                  **groups[2])
        hg_s.append(s_hg)
        ssm_s.append(s_ssm)
        conv_s.append(s_conv)

        merged = _merge(o_a, o_b, w_proj_a, w_proj_b, proj, l)
        x, xb = _out_ln(merged, w_out, x, ln1_g, ln1_b, l)
        hid, f_s, f_p = _ffn_up(xb, ffn_w_up, ffn_conv_w, ffn_conv_b, state_ffn_conv, l, batch=batch, seq=seq,
                                slot_rows=dec_seq, meta_slot=row_meta // dec_seq)
        ffn_s.append(f_s)
        ffn_p.append(f_p)
        if l < DEPTH - 1:
            x, xb = _ffn_down(hid, w_down_b, x, ln2_g, ln2_b, l, final=False, n_sample=n_sample,
                              row_prompt=row_prompt)
        else:
            y_sample, y_prompt = _ffn_down(hid, w_down_b, x, ln2_g, ln2_b, l, final=True, n_sample=n_sample,
                                           row_prompt=row_prompt)
    return (y_prompt.reshape(batch, seq, D_MODEL), y_sample.reshape(dec_batch, dec_seq, D_MODEL),
            jnp.stack(hg_p), jnp.stack(ssm_pp), jnp.stack(conv_p), jnp.stack(ffn_p),
            jnp.stack(hg_s), jnp.stack(ssm_s), jnp.stack(conv_s), jnp.stack(ffn_s))
```

```python
import functools
import math

import jax
import jax.numpy as jnp
from jax import lax
from jax.experimental import pallas as pl
from jax.experimental.pallas import tpu as pltpu

F32 = jnp.float32
BF16 = jnp.bfloat16

D_MODEL = 2048
DEPTH = 2
N_META = 16
HG_HEADS = 16
HG_D = 128
SSM_HEADS = 32
SSM_P = 64
SSM_GROUPS = 4
SSM_HPG = SSM_HEADS // SSM_GROUPS
SSM_N = 128
SSM_INNER = SSM_HEADS * SSM_P
SSM_GW = SSM_HPG * SSM_P
SSM_BC = 2 * SSM_GROUPS * SSM_N
SSM_CONV = 4
SSM_XBC = SSM_INNER + SSM_BC
D_FF = 5632
FFN_CONV = 3
ALPHA = (2.0 * DEPTH) ** 0.25
LN_EPS = 1e-5
RMS_EPS = 1e-6

LANES = 128
SUBLANES = 8
CHUNK = 128
N_LEVELS = int(math.log2(CHUNK))
TM = 512
TM_LN = 256
TM_PROJ_MAX = 1152
VMEM_LIMIT = 48 * 1024 * 1024
NEG_BIG = -1e30
LOG2_E = 1.4426950408889634

COL_Q, COL_F, COL_I, COL_G, COL_Z = 0, 2048, 4096, 6144, 8192
COL_XS, COL_BC = 10240, 12288
N_MAIN = 13312
COL_DT = 13312
COL_GATES = COL_DT + SSM_HEADS
COL_GA, COL_GB = N_MAIN, N_MAIN + D_MODEL


def _params(*semantics):
    return pltpu.CompilerParams(dimension_semantics=semantics, vmem_limit_bytes=VMEM_LIMIT)


def _nt(a, b):
    return lax.dot_general(a, b, (((1,), (1,)), ((), ())), preferred_element_type=F32)


def _dot(a, b):
    return jnp.dot(a, b, preferred_element_type=F32)


def _cumsum_rows(tri_b, g):
    n = g.shape[1]
    hi = g.astype(BF16)
    lo = (g - hi.astype(F32)).astype(BF16)
    s = _dot(tri_b, jnp.concatenate([hi, lo], axis=1))
    return s[:, :n] + s[:, n:]


def _silu(x):
    return x * jax.nn.sigmoid(x)


def _layer_norm(h, g, b):
    mu = jnp.mean(h, axis=-1, keepdims=True)
    d = h - mu
    var = jnp.mean(d * d, axis=-1, keepdims=True)
    return d * lax.rsqrt(var + LN_EPS) * g + b


def _pad_rows(x, rows):
    if x.shape[0] == rows:
        return x
    return jnp.concatenate([x, jnp.zeros((rows - x.shape[0],) + x.shape[1:], x.dtype)], axis=0)


def _lane_pair(row, h0):
    lane = lax.broadcasted_iota(jnp.int32, (1, LANES), 1)
    return jnp.where(lane < SSM_P, jnp.broadcast_to(row[:, h0:h0 + 1], (1, LANES)),
                     jnp.broadcast_to(row[:, h0 + 1:h0 + 2], (1, LANES)))


def _cast_kernel(x_ref, o_ref):
    o_ref[...] = x_ref[...].astype(o_ref.dtype)


def _cast_bf16(w, rows=512):
    lead, r, c = w.shape
    return pl.pallas_call(
        _cast_kernel,
        grid=(lead, r // rows),
        in_specs=[pl.BlockSpec((None, rows, c), lambda l, i: (l, i, 0))],
        out_specs=pl.BlockSpec((None, rows, c), lambda l, i: (l, i, 0)),
        out_shape=jax.ShapeDtypeStruct(w.shape, BF16),
        compiler_params=_params("arbitrary", "arbitrary"),
        name="cast_bf16",
    )(w)


def _in_proj_kernel(x_ref, w_ref, o_ref, wb_ref):
    @pl.when(pl.program_id(1) == 0)
    def _():
        wb_ref[...] = w_ref[0].T.astype(BF16)

    o_ref[...] = _dot(x_ref[...], wb_ref[...])


def _row_tile(m, cap):
    return max(t for t in range(16, cap + 1, 16) if m % t == 0)


def _in_proj(xb, w_in_t, layer, tn=1024):
    m, k = xb.shape
    tm = _row_tile(m, TM_PROJ_MAX)
    n_main = N_MAIN // tn
    n_tiles = n_main + 2 * D_MODEL // tn
    assert COL_GATES % SUBLANES == 0 and tn % SUBLANES == 0

    def window(j, i):
        row8 = jnp.where(j < n_main, j * (tn // SUBLANES), COL_GATES // SUBLANES + (j - n_main) * (tn // SUBLANES))
        return (layer, SUBLANES * row8, 0)

    return pl.pallas_call(
        _in_proj_kernel,
        grid=(n_tiles, m // tm),
        in_specs=[pl.BlockSpec((tm, k), lambda j, i: (i, 0)),
                  pl.BlockSpec((pl.Element(1), pl.Element(tn), pl.Element(k)), window)],
        out_specs=pl.BlockSpec((tm, tn), lambda j, i: (i, j)),
        out_shape=jax.ShapeDtypeStruct((m, n_tiles * tn), F32),
        scratch_shapes=[pltpu.VMEM((k, tn), BF16)],
        compiler_params=_params("arbitrary", "arbitrary"),
        name="in_proj",
    )(xb, w_in_t)


def _dt_kernel(x_ref, w_ref, b_ref, o_ref, wb_ref, *, layer):
    @pl.when(pl.program_id(0) == 0)
    def _():
        wb_ref[...] = w_ref[...].T.astype(BF16)

    raw = _dot(x_ref[...], wb_ref[...]) + b_ref[layer:layer + 1, :]
    o_ref[...] = jnp.maximum(raw, 0.0) + jnp.log1p(jnp.exp(-jnp.abs(raw)))


def _dt_proj(xb, w_in_t, dt_bias, layer):
    m, k = xb.shape
    return pl.pallas_call(
        functools.partial(_dt_kernel, layer=layer),
        grid=(m // TM,),
        in_specs=[pl.BlockSpec((TM, k), lambda i: (i, 0)),
                  pl.BlockSpec((None, LANES, k), lambda i: (layer, COL_DT // LANES, 0)),
                  pl.BlockSpec((DEPTH, LANES), lambda i: (0, 0))],
        out_specs=pl.BlockSpec((TM, LANES), lambda i: (i, 0)),
        out_shape=jax.ShapeDtypeStruct((m, LANES), F32),
        scratch_shapes=[pltpu.VMEM((k, LANES), BF16)],
        compiler_params=_params("arbitrary"),
        name="dt_proj",
    )(xb, w_in_t, dt_bias)


def _level_boundary(cum, h):
    c = cum.shape[0]
    if h >= SUBLANES:
        blk = cum.reshape(c // (2 * h), 2 * h, LANES)
        return jnp.broadcast_to(blk[:, h - 1:h, :], blk.shape).reshape(c, LANES)
    tiles = cum.reshape(c // SUBLANES, SUBLANES, LANES)
    sub = lax.broadcasted_iota(jnp.int32, tiles.shape, 1)
    out = None
    for start in range(0, SUBLANES, 2 * h):
        row = jnp.broadcast_to(tiles[:, start + h - 1:start + h, :], tiles.shape)
        out = row if out is None else jnp.where(sub >= start, row, out)
    return out.reshape(c, LANES)


def _gla_kernel(q_ref, f_ref, i_ref, g_ref, lbl_ref, nw_ref, s0_ref, tri_ref, lvl_ref, buf_ref,
                o_ref, s_out_ref, st_ref, a_ref, *, layer, heads, rows):
    del buf_ref
    c = pl.program_id(2)
    last = pl.num_programs(2) - 1

    @pl.when(c == 0)
    def _():
        for h in range(heads):
            st_ref[h] = s0_ref[0, h].T

    logits = lbl_ref[...]
    ex = jnp.exp(logits - jnp.max(logits, axis=0, keepdims=True))
    sm = ex / jnp.sum(ex, axis=0, keepdims=True)
    lb_all = jnp.zeros((1, sm.shape[1]), F32)
    for l in range(1, layer + 1):
        lb_all = lb_all + sm[l:l + 1]

    tri = tri_ref[...]
    lvl = lvl_ref[...]
    valid = lax.broadcasted_iota(jnp.int32, (CHUNK, LANES), 0) < rows

    head_cols = [slice(h * HG_D, (h + 1) * HG_D) for h in range(heads)]
    qs, ks, vs, decays, cums, o_inters = [], [], [], [], [], []
    for h, cols in enumerate(head_cols):
        lb = lb_all[:, cols]
        q = _silu(_pad_rows(q_ref[:, cols], CHUNK))
        f = lb + (1.0 - lb) * jax.nn.sigmoid(_pad_rows(f_ref[:, cols], CHUNK))
        decay = f if rows == CHUNK else jnp.where(valid, f, 1.0)
        qs.append(q)
        ks.append(1.0 - f)
        decays.append(decay)
        cums.append(_cumsum_rows(tri, jnp.log2(decay)))

    for h, cols in enumerate(head_cols):
        q, k, cum = qs[h], ks[h], cums[h]
        v32 = _pad_rows(i_ref[:, cols], CHUNK)
        vs.append(v32.astype(BF16))
        st = st_ref[h]
        o_inters.append(_nt((q * jnp.exp2(cum)).astype(BF16), st.astype(BF16)))
        c_last = cum[CHUNK - 1:CHUNK, :]
        k_dec = k * jnp.exp2(c_last - cum)
        st_ref[h] = st * jnp.exp2(c_last) + _dot(v32.T.astype(BF16), k_dec.astype(BF16))

    kbs = [k.astype(BF16) for k in ks]
    for h in range(heads):
        a_ref[h] = jnp.where(lvl == N_LEVELS, _nt(qs[h].astype(BF16), kbs[h]), 0.0)
    for h in range(heads):
        pltpu.store(a_ref.at[h], _nt((qs[h] * decays[h]).astype(BF16), kbs[h]), mask=lvl == 0)
    for lev in range(1, N_LEVELS):
        half = 1 << lev
        for h in range(heads):
            q, k, cum = qs[h], ks[h], cums[h]
            if half < SUBLANES:
                z = jnp.exp2(-jnp.abs(cum - _level_boundary(cum, half)))
                pltpu.store(a_ref.at[h], _nt((q * z).astype(BF16), (k * z).astype(BF16)), mask=lvl == lev)
            else:
                q_parts, k_parts, uppers = [], [], []
                for start in range(0, CHUNK, 2 * half):
                    lo, up = slice(start, start + half), slice(start + half, start + 2 * half)
                    mid = cum[start + half - 1:start + half, :]
                    q_parts.append(q[up] * jnp.exp2(cum[up] - mid))
                    k_parts += [k[lo] * jnp.exp2(mid - cum[lo]), k[up]]
                    uppers.append(up)
                p = _nt(jnp.concatenate(q_parts, axis=0).astype(BF16),
                        jnp.concatenate(k_parts, axis=0).astype(BF16))
                for n, up in enumerate(uppers):
                    pltpu.store(a_ref.at[h, up, :], p[n * half:(n + 1) * half, :], mask=lvl[up, :] == lev)

    for h, cols in enumerate(head_cols):
        o = (_dot(a_ref[h].astype(BF16), vs[h]) + o_inters[h])[:rows]
        o = o * lax.rsqrt(jnp.mean(o * o, axis=-1, keepdims=True) + RMS_EPS) * nw_ref[layer:layer + 1, cols]
        o_ref[:, cols] = (o * _silu(g_ref[:, cols])).astype(o_ref.dtype)

    @pl.when(c == last)
    def _():
        for h in range(heads):
            s_out_ref[0, h] = st_ref[h].T


def _gla(proj, buf, lb_logits, norm_w, s0, consts, *, layer, batch, seq, row0, heads_per_step=16):
    rows = min(seq, CHUNK)
    nc = seq // rows
    blk0 = row0 // rows
    hp = heads_per_step
    w = hp * HG_D

    def col(base):
        return lambda b, hb, c: (blk0 + b * nc + c, base // w + hb)

    par = lambda b, hb, c: (0, hb)
    state = lambda b, hb, c: (b, hb, 0, 0)
    out_rows = lambda b, hb, c: (blk0 + b * nc + c, hb)
    return pl.pallas_call(
        functools.partial(_gla_kernel, layer=layer, heads=hp, rows=rows),
        grid=(batch, HG_HEADS // hp, nc),
        in_specs=[pl.BlockSpec((rows, w), col(COL_Q)),
                  pl.BlockSpec((rows, w), col(COL_F)),
                  pl.BlockSpec((rows, w), col(COL_I)),
                  pl.BlockSpec((rows, w), col(COL_G)),
                  pl.BlockSpec((DEPTH, w), par),
                  pl.BlockSpec((DEPTH, w), par),
                  pl.BlockSpec((1, hp, HG_D, HG_D), state),
                  pl.BlockSpec((CHUNK, CHUNK), lambda b, hb, c: (0, 0)),
                  pl.BlockSpec((CHUNK, CHUNK), lambda b, hb, c: (0, 0)),
                  pl.BlockSpec((rows, w), out_rows)],
        out_specs=[pl.BlockSpec((rows, w), out_rows),
                   pl.BlockSpec((1, hp, HG_D, HG_D), state)],
        out_shape=[jax.ShapeDtypeStruct(buf.shape, buf.dtype),
                   jax.ShapeDtypeStruct((batch, HG_HEADS, HG_D, HG_D), F32)],
        scratch_shapes=[pltpu.VMEM((hp, HG_D, HG_D), F32), pltpu.VMEM((hp, CHUNK, CHUNK), F32)],
        input_output_aliases={9: 0},
        compiler_params=_params("arbitrary", "arbitrary", "arbitrary"),
        name="hgrn2",
    )(proj, proj, proj, proj, lb_logits, norm_w, s0, consts["tri"], consts["level"], buf)


def _causal_conv4(raw, ext_ref, w, bias):
    n = raw.shape[0]
    ext_ref[SUBLANES:SUBLANES + n, :] = raw
    out = (bias + w[3:4] * raw + w[2:3] * ext_ref[7:7 + n, :] + w[1:2] * ext_ref[6:6 + n, :]
           + w[0:1] * ext_ref[5:5 + n, :])
    ext_ref[0:SUBLANES, :] = raw[n - SUBLANES:]
    return out


def _ssd_kernel(xs_ref, bc_ref, z_ref, dt_ref, cw_ref, cb_ref, st_ref, alog_ref, dskip_ref, nw_ref,
                h0_ref, tri_ref, buf_ref, y_ref, h_out_ref, st_out_ref, ht_ref, px_ref, pbc_ref,
                *, layer, rows):
    del buf_ref
    c = pl.program_id(1)
    last = pl.num_programs(1) - 1
    tail = SSM_CONV - 1

    @pl.when(c == 0)
    def _():
        ht_ref[...] = h0_ref[0].T
        buf = jnp.concatenate([jnp.zeros((SUBLANES - tail, SSM_XBC), F32), st_ref[0]], axis=0)
        px_ref[0:SUBLANES, :] = buf[:, :SSM_INNER]
        pbc_ref[0:SUBLANES, :] = buf[:, SSM_INNER:]

    raw_x, raw_bc = xs_ref[...], bc_ref[...]
    st_out_ref[0, :, :SSM_INNER] = raw_x[rows - tail:]
    st_out_ref[0, :, SSM_INNER:] = raw_bc[rows - tail:]
    cw = cw_ref[...]
    cb = cb_ref[layer:layer + 1, :]
    xs = _pad_rows(_silu(_causal_conv4(raw_x, px_ref, cw[:, :SSM_INNER], cb[:, :SSM_INNER])), CHUNK)
    bc = _pad_rows(_silu(_causal_conv4(raw_bc, pbc_ref, cw[:, SSM_INNER:], cb[:, SSM_INNER:])), CHUNK)

    dt = _pad_rows(dt_ref[...], CHUNK)
    if rows < CHUNK:
        dt = jnp.where(lax.broadcasted_iota(jnp.int32, dt.shape, 0) < rows, dt, 0.0)
    a = dt * (-LOG2_E * jnp.exp(alog_ref[layer:layer + 1, :]))
    cum = _cumsum_rows(tri_ref[...], a)
    c_last = cum[CHUNK - 1:CHUNK, :]
    cum_t = cum.T
    dt_t = dt.T
    wgt_t = (jnp.exp2(c_last - cum) * dt).T
    dec = jnp.exp2(c_last)
    dskip = dskip_ref[layer:layer + 1, :]

    causal = (lax.broadcasted_iota(jnp.int32, (CHUNK, CHUNK), 0)
              >= lax.broadcasted_iota(jnp.int32, (CHUNK, CHUNK), 1))
    low = lax.broadcasted_iota(jnp.int32, (CHUNK, LANES), 1) < SSM_P
    z = z_ref[...]
    for g in range(SSM_GROUPS):
        bm = bc[:, g * SSM_N:(g + 1) * SSM_N]
        cm = bc[:, (SSM_GROUPS + g) * SSM_N:(SSM_GROUPS + g + 1) * SSM_N]
        cb_ts = _nt(cm.astype(BF16), bm.astype(BF16))
        bm_t = bm.T
        y_parts = []
        for pair in range(SSM_HPG // 2):
            h0 = g * SSM_HPG + 2 * pair
            cols = slice((h0 // 2) * LANES, (h0 // 2 + 1) * LANES)
            xp = xs[:, cols]
            ht = ht_ref[:, cols]
            lhs, rhs, upd_l, upd_r = [], [], [], []
            for h, mine in ((h0, low), (h0 + 1, jnp.logical_not(low))):
                col_b = jnp.broadcast_to(cum[:, h:h + 1], (CHUNK, CHUNK))
                lmat = jnp.exp2(jnp.where(causal, col_b - cum_t[h:h + 1, :], NEG_BIG))
                lhs.append((cb_ts * lmat * dt_t[h:h + 1, :]).astype(BF16))
                lhs.append((cm * jnp.exp2(col_b)).astype(BF16))
                xh = jnp.where(mine, xp, 0.0).astype(BF16)
                rhs.append(xh)
                rhs.append(jnp.where(mine, ht, 0.0).astype(BF16))
                upd_l.append((bm_t * wgt_t[h:h + 1, :]).astype(BF16))
                upd_r.append(xh)
            y_parts.append(_dot(jnp.concatenate(lhs, axis=1), jnp.concatenate(rhs, axis=0)))
            ht_ref[:, cols] = ht * _lane_pair(dec, h0) + _dot(jnp.concatenate(upd_l, axis=1),
                                                             jnp.concatenate(upd_r, axis=0))
            y_parts[-1] = y_parts[-1] + _lane_pair(dskip, h0) * xp
        gcols = slice(g * SSM_GW, (g + 1) * SSM_GW)
        y = jnp.concatenate(y_parts, axis=1)[:rows] * _silu(z[:, gcols])
        y = y * lax.rsqrt(jnp.mean(y * y, axis=-1, keepdims=True) + RMS_EPS) * nw_ref[layer:layer + 1, gcols]
        y_ref[:, gcols] = y.astype(y_ref.dtype)

    @pl.when(c == last)
    def _():
        h_out_ref[0] = ht_ref[...].T


def _ssd(proj, dt, buf, p, h0, conv0, consts, *, layer, batch, seq, row0):
    rows = min(seq, CHUNK)
    nc = seq // rows
    blk0 = row0 // rows
    n = SSM_N
    tail = SSM_CONV - 1
    h0 = h0.reshape(batch, SSM_INNER, n)
    row = lambda base, width: (lambda b, c: (blk0 + b * nc + c, base // width))
    par = lambda b, c: (0, 0)
    st = lambda b, c: (b, 0, 0)
    y, h_out, conv_out = pl.pallas_call(
        functools.partial(_ssd_kernel, layer=layer, rows=rows),
        grid=(batch, nc),
        in_specs=[pl.BlockSpec((rows, SSM_INNER), row(COL_XS, SSM_INNER)),
                  pl.BlockSpec((rows, SSM_BC), row(COL_BC, SSM_BC)),
                  pl.BlockSpec((rows, SSM_INNER), row(COL_Z, SSM_INNER)),
                  pl.BlockSpec((rows, LANES), row(0, LANES)),
                  pl.BlockSpec((None, SSM_CONV, SSM_XBC), lambda b, c: (layer, 0, 0)),
                  pl.BlockSpec((DEPTH, SSM_XBC), par),
                  pl.BlockSpec((1, tail, SSM_XBC), st),
                  pl.BlockSpec((DEPTH, LANES), par),
                  pl.BlockSpec((DEPTH, LANES), par),
                  pl.BlockSpec((DEPTH, SSM_INNER), par),
                  pl.BlockSpec((1, SSM_INNER, n), st),
                  pl.BlockSpec((CHUNK, CHUNK), par),
                  pl.BlockSpec((rows, SSM_INNER), row(0, SSM_INNER))],
        out_specs=[pl.BlockSpec((rows, SSM_INNER), row(0, SSM_INNER)),
                   pl.BlockSpec((1, SSM_INNER, n), st),
                   pl.BlockSpec((1, tail, SSM_XBC), st)],
        out_shape=[jax.ShapeDtypeStruct(buf.shape, buf.dtype),
                   jax.ShapeDtypeStruct((batch, SSM_INNER, n), F32),
                   jax.ShapeDtypeStruct((batch, tail, SSM_XBC), F32)],
        scratch_shapes=[pltpu.VMEM((n, SSM_INNER), F32),
                        pltpu.VMEM((SUBLANES + rows, SSM_INNER), F32),
                        pltpu.VMEM((SUBLANES + rows, SSM_BC), F32)],
        input_output_aliases={12: 0},
        compiler_params=_params("arbitrary", "arbitrary"),
        name="ssd",
    )(proj, proj, proj, dt, p["ssm_conv_w"], p["ssm_conv_b"], conv0, p["a_log"], p["d_skip"],
      p["ssm_norm_w"], h0, consts["tri"], buf)
    return y, h_out.reshape(batch, SSM_HEADS, SSM_P, n), conv_out


def _merge_kernel(oa_ref, ob_ref, wa_ref, wb_ref, ga_ref, gb_ref, o_ref, wab_ref, wbb_ref):
    @pl.when(pl.program_id(1) == 0)
    def _():
        wab_ref[...] = wa_ref[...].astype(BF16)
        wbb_ref[...] = wb_ref[...].astype(BF16)

    o_ref[...] = (jax.nn.sigmoid(ga_ref[...]) * _dot(oa_ref[...], wab_ref[...])
                  + jax.nn.sigmoid(gb_ref[...]) * _dot(ob_ref[...], wbb_ref[...])).astype(o_ref.dtype)


def _merge(o_a, o_b, w_a, w_b, proj, layer, tn=1024):
    m, k = o_a.shape
    wspec = pl.BlockSpec((None, k, tn), lambda j, i: (layer, 0, j), pipeline_mode=pl.Buffered(1))
    return pl.pallas_call(
        _merge_kernel,
        grid=(D_MODEL // tn, m // TM),
        in_specs=[pl.BlockSpec((TM, k), lambda j, i: (i, 0)),
                  pl.BlockSpec((TM, k), lambda j, i: (i, 0)),
                  wspec, wspec,
                  pl.BlockSpec((TM, tn), lambda j, i: (i, COL_GA // tn + j)),
                  pl.BlockSpec((TM, tn), lambda j, i: (i, COL_GB // tn + j))],
        out_specs=pl.BlockSpec((TM, tn), lambda j, i: (i, j)),
        out_shape=jax.ShapeDtypeStruct((m, D_MODEL), BF16),
        scratch_shapes=[pltpu.VMEM((k, tn), BF16), pltpu.VMEM((k, tn), BF16)],
        compiler_params=_params("arbitrary", "arbitrary"),
        name="merge",
    )(o_a, o_b, w_a, w_b, proj, proj)


def _out_ln_kernel(mg_ref, w_ref, x_ref, g_ref, b_ref, o_ref, ob_ref, wb_ref, *, layer):
    @pl.when(pl.program_id(0) == 0)
    def _():
        wb_ref[...] = w_ref[...].astype(BF16)

    h = ALPHA * x_ref[...] + _dot(mg_ref[...], wb_ref[...])
    y = _layer_norm(h, g_ref[layer:layer + 1, :], b_ref[layer:layer + 1, :])
    o_ref[...] = y
    ob_ref[...] = y.astype(ob_ref.dtype)


def _out_ln(merged, w_out, x, g, b, layer):
    m, k = merged.shape
    row = lambda i: (i, 0)
    fix = lambda i: (0, 0)
    return pl.pallas_call(
        functools.partial(_out_ln_kernel, layer=layer),
        grid=(m // TM_LN,),
        in_specs=[pl.BlockSpec((TM_LN, k), row),
                  pl.BlockSpec((None, k, D_MODEL), lambda i: (layer, 0, 0), pipeline_mode=pl.Buffered(1)),
                  pl.BlockSpec((TM_LN, D_MODEL), row),
                  pl.BlockSpec((DEPTH, D_MODEL), fix),
                  pl.BlockSpec((DEPTH, D_MODEL), fix)],
        out_specs=[pl.BlockSpec((TM_LN, D_MODEL), row), pl.BlockSpec((TM_LN, D_MODEL), row)],
        out_shape=[jax.ShapeDtypeStruct((m, D_MODEL), F32), jax.ShapeDtypeStruct((m, D_MODEL), BF16)],
        scratch_shapes=[pltpu.VMEM((k, D_MODEL), BF16)],
        compiler_params=_params("arbitrary"),
        name="out_ln",
    )(merged, w_out, x, g, b)


def _gelu(x):
    return 0.5 * x * (1.0 + lax.erf(x * (1.0 / math.sqrt(2.0))))


def _ffn_up_kernel(x_ref, wa_ref, wv_ref, cw_ref, cb_ref, st_ref, h_ref, st_s_ref, st_p_ref,
                   wab_ref, wvb_ref, prev_ref, meta_ref, *, layer, slot_rows, meta_slot, tiles_per_seq):
    i = pl.program_id(1)
    tn = wab_ref.shape[1]
    tail = FFN_CONV - 1

    @pl.when(i == 0)
    def _():
        wab_ref[...] = wa_ref[...].astype(BF16)
        wvb_ref[...] = wv_ref[...].astype(BF16)

    x = x_ref[...]
    a = _dot(x, wab_ref[...])
    w = cw_ref[...]
    bias = cb_ref[layer:layer + 1, :]

    def gated(conv):
        act = _gelu(conv).reshape(TM, tn)
        return (act * _dot(x, wvb_ref[...])).astype(h_ref.dtype)

    @pl.when(i == 0)
    def _():
        slots = TM // slot_rows
        n_s = st_ref.shape[0]
        a3 = a.reshape(slots, slot_rows, tn)
        prev = jnp.concatenate(
            [jnp.concatenate([jnp.zeros((n_s, SUBLANES - tail, tn), F32), st_ref[...]], axis=1),
             jnp.zeros((slots - n_s, SUBLANES, tn), F32)], axis=0)
        ext = jnp.concatenate([prev, a3], axis=1)
        n = slot_rows
        conv = bias + w[2:3] * a3 + w[1:2] * ext[:, 7:7 + n] + w[0:1] * ext[:, 6:6 + n]
        st_s_ref[...] = a3[:n_s, n - tail:]
        meta_ref[...] = a3[meta_slot, N_META - tail:N_META]
        h_ref[...] = gated(conv)

    @pl.when(i > 0)
    def _():
        @pl.when((i - 1) % tiles_per_seq == 0)
        def _():
            prev_ref[...] = jnp.concatenate([jnp.zeros((SUBLANES - tail, tn), F32), meta_ref[...]], axis=0)

        ext = jnp.concatenate([prev_ref[...], a], axis=0)
        conv = bias + w[2:3] * a + w[1:2] * ext[7:7 + TM] + w[0:1] * ext[6:6 + TM]
        prev_ref[...] = a[TM - SUBLANES:]
        st_p_ref[0] = a[TM - tail:]
        h_ref[...] = gated(conv)


def _ffn_up(xb, w_up, conv_w, conv_b, st_sample, layer, *, batch, seq, slot_rows, meta_slot, tn=512):
    m, k = xb.shape
    nj = D_FF // tn
    tail = FFN_CONV - 1
    tps = seq // TM
    n_s = st_sample.shape[1]
    return pl.pallas_call(
        functools.partial(_ffn_up_kernel, layer=layer, slot_rows=slot_rows, meta_slot=meta_slot,
                          tiles_per_seq=tps),
        grid=(nj, m // TM),
        in_specs=[pl.BlockSpec((TM, k), lambda j, i: (i, 0)),
                  pl.BlockSpec((None, k, tn), lambda j, i: (layer, 0, j)),
                  pl.BlockSpec((None, k, tn), lambda j, i: (layer, 0, nj + j)),
                  pl.BlockSpec((None, FFN_CONV, tn), lambda j, i: (layer, 0, j)),
                  pl.BlockSpec((DEPTH, tn), lambda j, i: (0, j)),
                  pl.BlockSpec((None, n_s, tail, tn), lambda j, i: (layer, 0, 0, j))],
        out_specs=[pl.BlockSpec((TM, tn), lambda j, i: (i, j)),
                   pl.BlockSpec((n_s, tail, tn), lambda j, i: (0, 0, j)),
                   pl.BlockSpec((1, tail, tn), lambda j, i: (jnp.maximum(i - 1, 0) // tps, 0, j))],
        out_shape=[jax.ShapeDtypeStruct((m, D_FF), BF16),
                   jax.ShapeDtypeStruct((n_s, tail, D_FF), F32),
                   jax.ShapeDtypeStruct((batch, tail, D_FF), F32)],
        scratch_shapes=[pltpu.VMEM((k, tn), BF16), pltpu.VMEM((k, tn), BF16),
                        pltpu.VMEM((SUBLANES, tn), F32), pltpu.VMEM((tail, tn), F32)],
        compiler_params=_params("arbitrary", "arbitrary"),
        name="ffn_up",
    )(xb, w_up, w_up, conv_w, conv_b, st_sample)


def _ffn_down_kernel(h_ref, w_ref, x_ref, g_ref, b_ref, o_ref, ob_ref, *, layer):
    y = _layer_norm(ALPHA * x_ref[...] + _dot(h_ref[...], w_ref[...]),
                    g_ref[layer:layer + 1, :], b_ref[layer:layer + 1, :])
    o_ref[...] = y
    ob_ref[...] = y.astype(ob_ref.dtype)


def _ffn_down_final_kernel(h_ref, w_ref, x_ref, g_ref, b_ref, ys_ref, yp_ref, *, layer, prompt_tile0):
    i = pl.program_id(0)
    y = _layer_norm(ALPHA * x_ref[...] + _dot(h_ref[...], w_ref[...]),
                    g_ref[layer:layer + 1, :], b_ref[layer:layer + 1, :])

    @pl.when(i == 0)
    def _():
        ys_ref[...] = y

    @pl.when(i >= prompt_tile0)
    def _():
        yp_ref[...] = y


def _ffn_down(h, w_down_b, x, g, b, layer, *, final, n_sample, row_prompt):
    m, k = h.shape
    row = lambda i: (i, 0)
    fix = lambda i: (0, 0)
    in_specs = [pl.BlockSpec((TM_LN, k), row),
                pl.BlockSpec((None, k, D_MODEL), lambda i: (layer, 0, 0), pipeline_mode=pl.Buffered(1)),
                pl.BlockSpec((TM_LN, D_MODEL), row),
                pl.BlockSpec((DEPTH, D_MODEL), fix),
                pl.BlockSpec((DEPTH, D_MODEL), fix)]
    if not final:
        return pl.pallas_call(
            functools.partial(_ffn_down_kernel, layer=layer),
            grid=(m // TM_LN,),
            in_specs=in_specs,
            out_specs=[pl.BlockSpec((TM_LN, D_MODEL), row), pl.BlockSpec((TM_LN, D_MODEL), row)],
            out_shape=[jax.ShapeDtypeStruct((m, D_MODEL), F32), jax.ShapeDtypeStruct((m, D_MODEL), BF16)],
            compiler_params=_params("arbitrary"),
            name="ffn_down",
        )(h, w_down_b, x, g, b)
    assert n_sample == TM_LN and row_prompt % TM_LN == 0
    t0 = row_prompt // TM_LN
    return pl.pallas_call(
        functools.partial(_ffn_down_final_kernel, layer=layer, prompt_tile0=t0),
        grid=(m // TM_LN,),
        in_specs=in_specs,
        out_specs=[pl.BlockSpec((TM_LN, D_MODEL), fix),
                   pl.BlockSpec((TM_LN, D_MODEL), lambda i: (jnp.maximum(i - t0, 0), 0))],
        out_shape=[jax.ShapeDtypeStruct((n_sample, D_MODEL), F32),
                   jax.ShapeDtypeStruct((m - row_prompt, D_MODEL), F32)],
        compiler_params=_params("arbitrary"),
        name="ffn_down_final",
    )(h, w_down_b, x, g, b)


def _constants():
    t = jnp.arange(CHUNK)[:, None]
    s = jnp.arange(CHUNK)[None, :]
    level = jnp.full((CHUNK, CHUNK), -1, jnp.int32)
    for lev in range(N_LEVELS):
        level = jnp.where((t > s) & (((t ^ s) >> lev) == 1), lev, level)
    level = jnp.where(t == s, N_LEVELS, level)
    return dict(tri=(t >= s).astype(BF16), level=level)


def _head_lanes(v):
    return jnp.pad(v.astype(F32), ((0, 0), (0, LANES - SSM_HEADS)))


def kernel(x_prompt, x_sample, state_hgrn, state_ssm, state_ssm_conv, state_ffn_conv, meta_tokens, w_in,
           hgrn_lb_logits, hgrn_norm_w, w_proj_a, ssm_conv_w, ssm_conv_b, ssm_dt_bias, ssm_a_log, ssm_d,
           ssm_norm_w, w_proj_b, w_out, ln1_g, ln1_b, ffn_w_up, ffn_conv_w, ffn_conv_b, ffn_w_down, ln2_g, ln2_b):
    batch, seq, _ = x_prompt.shape
    dec_batch, dec_seq, _ = x_sample.shape
    n_sample = dec_batch * dec_seq
    row_meta = n_sample
    row_prompt = TM
    assert seq % TM == 0 and dec_seq >= N_META and row_meta + dec_seq <= TM and row_meta % dec_seq == 0
    m = row_prompt + batch * seq

    x = jnp.concatenate([x_sample.reshape(n_sample, D_MODEL).astype(F32), meta_tokens.astype(F32),
                         jnp.zeros((row_prompt - n_sample - N_META, D_MODEL), F32),
                         x_prompt.reshape(batch * seq, D_MODEL).astype(F32)], axis=0)
    xb = x.astype(BF16)
    consts = _constants()
    lb_logits = hgrn_lb_logits.astype(F32)
    ssm_p = dict(ssm_conv_w=ssm_conv_w, ssm_conv_b=ssm_conv_b, a_log=_head_lanes(ssm_a_log),
                 d_skip=_head_lanes(ssm_d), ssm_norm_w=ssm_norm_w)
    dt_bias = _head_lanes(ssm_dt_bias)
    w_down_b = _cast_bf16(ffn_w_down)
    groups = (dict(batch=1, seq=N_META, row0=row_meta), dict(batch=batch, seq=seq, row0=row_prompt),
              dict(batch=dec_batch, seq=dec_seq, row0=0))

    w_in_t = jnp.swapaxes(w_in, 1, 2)
    o_a = jnp.zeros((m, HG_HEADS * HG_D), BF16)
    o_b = jnp.zeros((m, SSM_INNER), BF16)

    hg_p, ssm_pp, conv_p, ffn_p, hg_s, ssm_s, conv_s, ffn_s = ([] for _ in range(8))
    y_sample = y_prompt = None
    for l in range(DEPTH):
        proj = _in_proj(xb, w_in_t, l)
        dt = _dt_proj(xb, w_in_t, dt_bias, l)

        o_a, hg_m = _gla(proj, o_a, lb_logits, hgrn_norm_w, jnp.zeros((1, HG_HEADS, HG_D, HG_D), F32), consts,
                         layer=l, **groups[0])
        o_b, ssm_m, conv_m = _ssd(proj, dt, o_b, ssm_p, jnp.zeros((1, SSM_HEADS, SSM_P, SSM_N), F32),
                                  jnp.zeros((1, SSM_CONV - 1, SSM_XBC), F32), consts, layer=l, **groups[0])
        seed = lambda s: jnp.broadcast_to(s, (batch,) + s.shape[1:])
        o_a, s_hg = _gla(proj, o_a, lb_logits, hgrn_norm_w, seed(hg_m), consts, layer=l, **groups[1])
        o_b, s_ssm, s_conv = _ssd(proj, dt, o_b, ssm_p, seed(ssm_m), seed(conv_m), consts, layer=l, **groups[1])
        hg_p.append(s_hg)
        ssm_pp.append(s_ssm)
        conv_p.append(s_conv)
        o_a, s_hg = _gla(proj, o_a, lb_logits, hgrn_norm_w, state_hgrn[l], consts, layer=l, **groups[2])
        o_b, s_ssm, s_conv = _ssd(proj, dt, o_b, ssm_p, state_ssm[l], state_ssm_conv[l], consts, layer=l,
                                  **groups[2])
        hg_s.append(s_hg)
        ssm_s.append(s_ssm)
        conv_s.append(s_conv)

        merged = _merge(o_a, o_b, w_proj_a, w_proj_b, proj, l)
        x, xb = _out_ln(merged, w_out, x, ln1_g, ln1_b, l)
        hid, f_s, f_p = _ffn_up(xb, ffn_w_up, ffn_conv_w, ffn_conv_b, state_ffn_conv, l, batch=batch, seq=seq,
                                slot_rows=dec_seq, meta_slot=row_meta // dec_seq)
        ffn_s.append(f_s)
        ffn_p.append(f_p)
        if l < DEPTH - 1:
            x, xb = _ffn_down(hid, w_down_b, x, ln2_g, ln2_b, l, final=False, n_sample=n_sample,
                              row_prompt=row_prompt)
        else:
            y_sample, y_prompt = _ffn_down(hid, w_down_b, x, ln2_g, ln2_b, l, final=True, n_sample=n_sample,
                                           row_prompt=row_prompt)
    return (y_prompt.reshape(batch, seq, D_MODEL), y_sample.reshape(dec_batch, dec_seq, D_MODEL),
            jnp.stack(hg_p), jnp.stack(ssm_pp), jnp.stack(conv_p), jnp.stack(ffn_p),
            jnp.stack(hg_s), jnp.stack(ssm_s), jnp.stack(conv_s), jnp.stack(ffn_s))
```

```python
import functools
import math

import jax
import jax.numpy as jnp
from jax import lax
from jax.experimental import pallas as pl
from jax.experimental.pallas import tpu as pltpu

F32 = jnp.float32
BF16 = jnp.bfloat16

D_MODEL = 2048
DEPTH = 2
N_META = 16
HG_HEADS = 16
HG_D = 128
SSM_HEADS = 32
SSM_P = 64
SSM_GROUPS = 4
SSM_HPG = SSM_HEADS // SSM_GROUPS
SSM_N = 128
SSM_INNER = SSM_HEADS * SSM_P
SSM_GW = SSM_HPG * SSM_P
SSM_BC = 2 * SSM_GROUPS * SSM_N
SSM_CONV = 4
SSM_XBC = SSM_INNER + SSM_BC
D_FF = 5632
FFN_CONV = 3
ALPHA = (2.0 * DEPTH) ** 0.25
LN_EPS = 1e-5
RMS_EPS = 1e-6

LANES = 128
SUBLANES = 8
CHUNK = 128
N_LEVELS = int(math.log2(CHUNK))
TM = 512
TM_LN = 256
TM_PROJ_MAX = 1152
FFN_KCHUNK = 256
VMEM_LIMIT = 56 * 1024 * 1024
NEG_BIG = -1e30
LOG2_E = 1.4426950408889634

COL_Q, COL_F, COL_I, COL_G, COL_Z = 0, 2048, 4096, 6144, 8192
COL_XS, COL_BC = 10240, 12288
N_MAIN = 13312
COL_DT = 13312
COL_GATES = COL_DT + SSM_HEADS
COL_GA, COL_GB = N_MAIN, N_MAIN + D_MODEL


def _params(*semantics):
    return pltpu.CompilerParams(dimension_semantics=semantics, vmem_limit_bytes=VMEM_LIMIT)


def _nt(a, b):
    return lax.dot_general(a, b, (((1,), (1,)), ((), ())), preferred_element_type=F32)


def _dot(a, b):
    return jnp.dot(a, b, preferred_element_type=F32)


def _cumsum_rows(tri_b, g):
    n = g.shape[1]
    hi = g.astype(BF16)
    lo = (g - hi.astype(F32)).astype(BF16)
    s = _dot(tri_b, jnp.concatenate([hi, lo], axis=1))
    return s[:, :n] + s[:, n:]


def _silu(x):
    return x * jax.nn.sigmoid(x)


def _layer_norm(h, g, b):
    mu = jnp.mean(h, axis=-1, keepdims=True)
    d = h - mu
    var = jnp.mean(d * d, axis=-1, keepdims=True)
    return d * lax.rsqrt(var + LN_EPS) * g + b


def _pad_rows(x, rows):
    if x.shape[0] == rows:
        return x
    return jnp.concatenate([x, jnp.zeros((rows - x.shape[0],) + x.shape[1:], x.dtype)], axis=0)


def _lane_pair(row, h0):
    lane = lax.broadcasted_iota(jnp.int32, (1, LANES), 1)
    return jnp.where(lane < SSM_P, jnp.broadcast_to(row[:, h0:h0 + 1], (1, LANES)),
                     jnp.broadcast_to(row[:, h0 + 1:h0 + 2], (1, LANES)))


def _cast_kernel(x_ref, o_ref):
    o_ref[...] = x_ref[...].astype(o_ref.dtype)


def _cast_bf16(w, rows=512):
    lead, r, c = w.shape
    return pl.pallas_call(
        _cast_kernel,
        grid=(lead, r // rows),
        in_specs=[pl.BlockSpec((None, rows, c), lambda l, i: (l, i, 0))],
        out_specs=pl.BlockSpec((None, rows, c), lambda l, i: (l, i, 0)),
        out_shape=jax.ShapeDtypeStruct(w.shape, BF16),
        compiler_params=_params("arbitrary", "arbitrary"),
        name="cast_bf16",
    )(w)


def _in_proj_kernel(x_ref, w_ref, o_ref, wb_ref):
    @pl.when(pl.program_id(1) == 0)
    def _():
        wb_ref[...] = w_ref[0].T.astype(BF16)

    o_ref[...] = _dot(x_ref[...], wb_ref[...])


def _row_tile(m, cap):
    return max(t for t in range(16, cap + 1, 16) if m % t == 0)


def _in_proj(xb, w_in_t, layer, tn=1024):
    m, k = xb.shape
    tm = _row_tile(m, TM_PROJ_MAX)
    n_main = N_MAIN // tn
    n_tiles = n_main + 2 * D_MODEL // tn
    assert COL_GATES % SUBLANES == 0 and tn % SUBLANES == 0

    def window(j, i):
        row8 = jnp.where(j < n_main, j * (tn // SUBLANES), COL_GATES // SUBLANES + (j - n_main) * (tn // SUBLANES))
        return (layer, SUBLANES * row8, 0)

    return pl.pallas_call(
        _in_proj_kernel,
        grid=(n_tiles, m // tm),
        in_specs=[pl.BlockSpec((tm, k), lambda j, i: (i, 0)),
                  pl.BlockSpec((pl.Element(1), pl.Element(tn), pl.Element(k)), window)],
        out_specs=pl.BlockSpec((tm, tn), lambda j, i: (i, j)),
        out_shape=jax.ShapeDtypeStruct((m, n_tiles * tn), F32),
        scratch_shapes=[pltpu.VMEM((k, tn), BF16)],
        compiler_params=_params("arbitrary", "arbitrary"),
        name="in_proj",
    )(xb, w_in_t)


def _dt_kernel(x_ref, w_ref, b_ref, o_ref, wb_ref, *, layer):
    @pl.when(pl.program_id(0) == 0)
    def _():
        wb_ref[...] = w_ref[...].T.astype(BF16)

    raw = _dot(x_ref[...], wb_ref[...]) + b_ref[layer:layer + 1, :]
    o_ref[...] = jnp.maximum(raw, 0.0) + jnp.log1p(jnp.exp(-jnp.abs(raw)))


def _dt_proj(xb, w_in_t, dt_bias, layer):
    m, k = xb.shape
    return pl.pallas_call(
        functools.partial(_dt_kernel, layer=layer),
        grid=(m // TM,),
        in_specs=[pl.BlockSpec((TM, k), lambda i: (i, 0)),
                  pl.BlockSpec((None, LANES, k), lambda i: (layer, COL_DT // LANES, 0)),
                  pl.BlockSpec((DEPTH, LANES), lambda i: (0, 0))],
        out_specs=pl.BlockSpec((TM, LANES), lambda i: (i, 0)),
        out_shape=jax.ShapeDtypeStruct((m, LANES), F32),
        scratch_shapes=[pltpu.VMEM((k, LANES), BF16)],
        compiler_params=_params("arbitrary"),
        name="dt_proj",
    )(xb, w_in_t, dt_bias)


def _level_boundary(cum, h):
    c = cum.shape[0]
    if h >= SUBLANES:
        blk = cum.reshape(c // (2 * h), 2 * h, LANES)
        return jnp.broadcast_to(blk[:, h - 1:h, :], blk.shape).reshape(c, LANES)
    tiles = cum.reshape(c // SUBLANES, SUBLANES, LANES)
    sub = lax.broadcasted_iota(jnp.int32, tiles.shape, 1)
    out = None
    for start in range(0, SUBLANES, 2 * h):
        row = jnp.broadcast_to(tiles[:, start + h - 1:start + h, :], tiles.shape)
        out = row if out is None else jnp.where(sub >= start, row, out)
    return out.reshape(c, LANES)


def _gla_kernel(q_ref, f_ref, i_ref, g_ref, lbl_ref, nw_ref, s0_ref, tri_ref, lvl_ref, buf_ref,
                o_ref, s_out_ref, st_ref, a_ref, *, layer, heads, rows):
    del buf_ref
    c = pl.program_id(2)
    last = pl.num_programs(2) - 1

    @pl.when(c == 0)
    def _():
        for h in range(heads):
            st_ref[h] = s0_ref[0, h].T

    logits = lbl_ref[...]
    ex = jnp.exp(logits - jnp.max(logits, axis=0, keepdims=True))
    sm = ex / jnp.sum(ex, axis=0, keepdims=True)
    lb_all = jnp.zeros((1, sm.shape[1]), F32)
    for l in range(1, layer + 1):
        lb_all = lb_all + sm[l:l + 1]

    tri = tri_ref[...]
    lvl = lvl_ref[...]
    valid = lax.broadcasted_iota(jnp.int32, (CHUNK, LANES), 0) < rows

    head_cols = [slice(h * HG_D, (h + 1) * HG_D) for h in range(heads)]
    qs, ks, vs, decays, cums, o_inters = [], [], [], [], [], []
    for h, cols in enumerate(head_cols):
        lb = lb_all[:, cols]
        q = _silu(_pad_rows(q_ref[:, cols], CHUNK))
        f = lb + (1.0 - lb) * jax.nn.sigmoid(_pad_rows(f_ref[:, cols], CHUNK))
        decay = f if rows == CHUNK else jnp.where(valid, f, 1.0)
        qs.append(q)
        ks.append(1.0 - f)
        decays.append(decay)
        cums.append(_cumsum_rows(tri, jnp.log2(decay)))

    for h, cols in enumerate(head_cols):
        q, k, cum = qs[h], ks[h], cums[h]
        v32 = _pad_rows(i_ref[:, cols], CHUNK)
        vs.append(v32.astype(BF16))
        st = st_ref[h]
        o_inters.append(_nt((q * jnp.exp2(cum)).astype(BF16), st.astype(BF16)))
        c_last = cum[CHUNK - 1:CHUNK, :]
        k_dec = k * jnp.exp2(c_last - cum)
        st_ref[h] = st * jnp.exp2(c_last) + _dot(v32.T.astype(BF16), k_dec.astype(BF16))

    kbs = [k.astype(BF16) for k in ks]
    for h in range(heads):
        a_ref[h] = jnp.where(lvl == N_LEVELS, _nt(qs[h].astype(BF16), kbs[h]), 0.0)
    for h in range(heads):
        pltpu.store(a_ref.at[h], _nt((qs[h] * decays[h]).astype(BF16), kbs[h]), mask=lvl == 0)
    for lev in range(1, N_LEVELS):
        half = 1 << lev
        for h in range(heads):
            q, k, cum = qs[h], ks[h], cums[h]
            if half < SUBLANES:
                z = jnp.exp2(-jnp.abs(cum - _level_boundary(cum, half)))
                pltpu.store(a_ref.at[h], _nt((q * z).astype(BF16), (k * z).astype(BF16)), mask=lvl == lev)
            else:
                q_parts, k_parts, uppers = [], [], []
                for start in range(0, CHUNK, 2 * half):
                    lo, up = slice(start, start + half), slice(start + half, start + 2 * half)
                    mid = cum[start + half - 1:start + half, :]
                    q_parts.append(q[up] * jnp.exp2(cum[up] - mid))
                    k_parts += [k[lo] * jnp.exp2(mid - cum[lo]), k[up]]
                    uppers.append(up)
                p = _nt(jnp.concatenate(q_parts, axis=0).astype(BF16),
                        jnp.concatenate(k_parts, axis=0).astype(BF16))
                for n, up in enumerate(uppers):
                    pltpu.store(a_ref.at[h, up, :], p[n * half:(n + 1) * half, :], mask=lvl[up, :] == lev)

    for h, cols in enumerate(head_cols):
        o = (_dot(a_ref[h].astype(BF16), vs[h]) + o_inters[h])[:rows]
        o = o * lax.rsqrt(jnp.mean(o * o, axis=-1, keepdims=True) + RMS_EPS) * nw_ref[layer:layer + 1, cols]
        o_ref[:, cols] = (o * _silu(g_ref[:, cols])).astype(o_ref.dtype)

    @pl.when(c == last)
    def _():
        for h in range(heads):
            s_out_ref[0, h] = st_ref[h].T


def _gla(proj, buf, lb_logits, norm_w, s0, consts, *, layer, batch, seq, row0, heads_per_step=16):
    rows = min(seq, CHUNK)
    nc = seq // rows
    blk0 = row0 // rows
    hp = heads_per_step
    w = hp * HG_D

    def col(base):
        return lambda b, hb, c: (blk0 + b * nc + c, base // w + hb)

    par = lambda b, hb, c: (0, hb)
    state = lambda b, hb, c: (b, hb, 0, 0)
    out_rows = lambda b, hb, c: (blk0 + b * nc + c, hb)
    return pl.pallas_call(
        functools.partial(_gla_kernel, layer=layer, heads=hp, rows=rows),
        grid=(batch, HG_HEADS // hp, nc),
        in_specs=[pl.BlockSpec((rows, w), col(COL_Q)),
                  pl.BlockSpec((rows, w), col(COL_F)),
                  pl.BlockSpec((rows, w), col(COL_I)),
                  pl.BlockSpec((rows, w), col(COL_G)),
                  pl.BlockSpec((DEPTH, w), par),
                  pl.BlockSpec((DEPTH, w), par),
                  pl.BlockSpec((1, hp, HG_D, HG_D), state),
                  pl.BlockSpec((CHUNK, CHUNK), lambda b, hb, c: (0, 0)),
                  pl.BlockSpec((CHUNK, CHUNK), lambda b, hb, c: (0, 0)),
                  pl.BlockSpec((rows, w), out_rows)],
        out_specs=[pl.BlockSpec((rows, w), out_rows),
                   pl.BlockSpec((1, hp, HG_D, HG_D), state)],
        out_shape=[jax.ShapeDtypeStruct(buf.shape, buf.dtype),
                   jax.ShapeDtypeStruct((batch, HG_HEADS, HG_D, HG_D), F32)],
        scratch_shapes=[pltpu.VMEM((hp, HG_D, HG_D), F32), pltpu.VMEM((hp, CHUNK, CHUNK), F32)],
        input_output_aliases={9: 0},
        compiler_params=_params("arbitrary", "arbitrary", "arbitrary"),
        name="hgrn2",
    )(proj, proj, proj, proj, lb_logits, norm_w, s0, consts["tri"], consts["level"], buf)


def _causal_conv4(raw, ext_ref, w, bias):
    n = raw.shape[0]
    ext_ref[SUBLANES:SUBLANES + n, :] = raw
    out = (bias + w[3:4] * raw + w[2:3] * ext_ref[7:7 + n, :] + w[1:2] * ext_ref[6:6 + n, :]
           + w[0:1] * ext_ref[5:5 + n, :])
    ext_ref[0:SUBLANES, :] = raw[n - SUBLANES:]
    return out


def _ssd_kernel(xs_ref, bc_ref, z_ref, dt_ref, cw_ref, cb_ref, st_ref, alog_ref, dskip_ref, nw_ref,
                h0_ref, tri_ref, buf_ref, y_ref, h_out_ref, st_out_ref, ht_ref, px_ref, pbc_ref,
                *, layer, rows):
    del buf_ref
    c = pl.program_id(1)
    last = pl.num_programs(1) - 1
    tail = SSM_CONV - 1

    @pl.when(c == 0)
    def _():
        ht_ref[...] = h0_ref[0].T
        buf = jnp.concatenate([jnp.zeros((SUBLANES - tail, SSM_XBC), F32), st_ref[0]], axis=0)
        px_ref[0:SUBLANES, :] = buf[:, :SSM_INNER]
        pbc_ref[0:SUBLANES, :] = buf[:, SSM_INNER:]

    raw_x, raw_bc = xs_ref[...], bc_ref[...]
    st_out_ref[0, :, :SSM_INNER] = raw_x[rows - tail:]
    st_out_ref[0, :, SSM_INNER:] = raw_bc[rows - tail:]
    cw = cw_ref[...]
    cb = cb_ref[layer:layer + 1, :]
    xs = _pad_rows(_silu(_causal_conv4(raw_x, px_ref, cw[:, :SSM_INNER], cb[:, :SSM_INNER])), CHUNK)
    bc = _pad_rows(_silu(_causal_conv4(raw_bc, pbc_ref, cw[:, SSM_INNER:], cb[:, SSM_INNER:])), CHUNK)

    dt = _pad_rows(dt_ref[...], CHUNK)
    if rows < CHUNK:
        dt = jnp.where(lax.broadcasted_iota(jnp.int32, dt.shape, 0) < rows, dt, 0.0)
    a = dt * (-LOG2_E * jnp.exp(alog_ref[layer:layer + 1, :]))
    cum = _cumsum_rows(tri_ref[...], a)
    c_last = cum[CHUNK - 1:CHUNK, :]
    cum_t = cum.T
    dt_t = dt.T
    wgt_t = (jnp.exp2(c_last - cum) * dt).T
    dec = jnp.exp2(c_last)
    dskip = dskip_ref[layer:layer + 1, :]

    causal = (lax.broadcasted_iota(jnp.int32, (CHUNK, CHUNK), 0)
              >= lax.broadcasted_iota(jnp.int32, (CHUNK, CHUNK), 1))
    low = lax.broadcasted_iota(jnp.int32, (CHUNK, LANES), 1) < SSM_P
    z = z_ref[...]
    for g in range(SSM_GROUPS):
        bm = bc[:, g * SSM_N:(g + 1) * SSM_N]
        cm = bc[:, (SSM_GROUPS + g) * SSM_N:(SSM_GROUPS + g + 1) * SSM_N]
        cb_ts = _nt(cm.astype(BF16), bm.astype(BF16))
        bm_t = bm.T
        y_parts = []
        for pair in range(SSM_HPG // 2):
            h0 = g * SSM_HPG + 2 * pair
            cols = slice((h0 // 2) * LANES, (h0 // 2 + 1) * LANES)
            xp = xs[:, cols]
            ht = ht_ref[:, cols]
            lhs, rhs, upd_l, upd_r = [], [], [], []
            for h, mine in ((h0, low), (h0 + 1, jnp.logical_not(low))):
                col_b = jnp.broadcast_to(cum[:, h:h + 1], (CHUNK, CHUNK))
                lmat = jnp.exp2(jnp.where(causal, col_b - cum_t[h:h + 1, :], NEG_BIG))
                lhs.append((cb_ts * lmat * dt_t[h:h + 1, :]).astype(BF16))
                lhs.append((cm * jnp.exp2(col_b)).astype(BF16))
                xh = jnp.where(mine, xp, 0.0).astype(BF16)
                rhs.append(xh)
                rhs.append(jnp.where(mine, ht, 0.0).astype(BF16))
                upd_l.append((bm_t * wgt_t[h:h + 1, :]).astype(BF16))
                upd_r.append(xh)
            y_parts.append(_dot(jnp.concatenate(lhs, axis=1), jnp.concatenate(rhs, axis=0)))
            ht_ref[:, cols] = ht * _lane_pair(dec, h0) + _dot(jnp.concatenate(upd_l, axis=1),
                                                             jnp.concatenate(upd_r, axis=0))
            y_parts[-1] = y_parts[-1] + _lane_pair(dskip, h0) * xp
        gcols = slice(g * SSM_GW, (g + 1) * SSM_GW)
        y = jnp.concatenate(y_parts, axis=1)[:rows] * _silu(z[:, gcols])
        y = y * lax.rsqrt(jnp.mean(y * y, axis=-1, keepdims=True) + RMS_EPS) * nw_ref[layer:layer + 1, gcols]
        y_ref[:, gcols] = y.astype(y_ref.dtype)

    @pl.when(c == last)
    def _():
        h_out_ref[0] = ht_ref[...].T


def _ssd(proj, dt, buf, p, h0, conv0, consts, *, layer, batch, seq, row0):
    rows = min(seq, CHUNK)
    nc = seq // rows
    blk0 = row0 // rows
    n = SSM_N
    tail = SSM_CONV - 1
    h0 = h0.reshape(batch, SSM_INNER, n)
    row = lambda base, width: (lambda b, c: (blk0 + b * nc + c, base // width))
    par = lambda b, c: (0, 0)
    st = lambda b, c: (b, 0, 0)
    y, h_out, conv_out = pl.pallas_call(
        functools.partial(_ssd_kernel, layer=layer, rows=rows),
        grid=(batch, nc),
        in_specs=[pl.BlockSpec((rows, SSM_INNER), row(COL_XS, SSM_INNER)),
                  pl.BlockSpec((rows, SSM_BC), row(COL_BC, SSM_BC)),
                  pl.BlockSpec((rows, SSM_INNER), row(COL_Z, SSM_INNER)),
                  pl.BlockSpec((rows, LANES), row(0, LANES)),
                  pl.BlockSpec((None, SSM_CONV, SSM_XBC), lambda b, c: (layer, 0, 0)),
                  pl.BlockSpec((DEPTH, SSM_XBC), par),
                  pl.BlockSpec((1, tail, SSM_XBC), st),
                  pl.BlockSpec((DEPTH, LANES), par),
                  pl.BlockSpec((DEPTH, LANES), par),
                  pl.BlockSpec((DEPTH, SSM_INNER), par),
                  pl.BlockSpec((1, SSM_INNER, n), st),
                  pl.BlockSpec((CHUNK, CHUNK), par),
                  pl.BlockSpec((rows, SSM_INNER), row(0, SSM_INNER))],
        out_specs=[pl.BlockSpec((rows, SSM_INNER), row(0, SSM_INNER)),
                   pl.BlockSpec((1, SSM_INNER, n), st),
                   pl.BlockSpec((1, tail, SSM_XBC), st)],
        out_shape=[jax.ShapeDtypeStruct(buf.shape, buf.dtype),
                   jax.ShapeDtypeStruct((batch, SSM_INNER, n), F32),
                   jax.ShapeDtypeStruct((batch, tail, SSM_XBC), F32)],
        scratch_shapes=[pltpu.VMEM((n, SSM_INNER), F32),
                        pltpu.VMEM((SUBLANES + rows, SSM_INNER), F32),
                        pltpu.VMEM((SUBLANES + rows, SSM_BC), F32)],
        input_output_aliases={12: 0},
        compiler_params=_params("arbitrary", "arbitrary"),
        name="ssd",
    )(proj, proj, proj, dt, p["ssm_conv_w"], p["ssm_conv_b"], conv0, p["a_log"], p["d_skip"],
      p["ssm_norm_w"], h0, consts["tri"], buf)
    return y, h_out.reshape(batch, SSM_HEADS, SSM_P, n), conv_out


def _merge_kernel(oa_ref, ob_ref, wa_ref, wb_ref, ga_ref, gb_ref, o_ref, wab_ref, wbb_ref):
    @pl.when(pl.program_id(1) == 0)
    def _():
        wab_ref[...] = wa_ref[...].astype(BF16)
        wbb_ref[...] = wb_ref[...].astype(BF16)

    o_ref[...] = (jax.nn.sigmoid(ga_ref[...]) * _dot(oa_ref[...], wab_ref[...])
                  + jax.nn.sigmoid(gb_ref[...]) * _dot(ob_ref[...], wbb_ref[...])).astype(o_ref.dtype)


def _merge(o_a, o_b, w_a, w_b, proj, layer, tn=1024):
    m, k = o_a.shape
    wspec = pl.BlockSpec((None, k, tn), lambda j, i: (layer, 0, j), pipeline_mode=pl.Buffered(1))
    return pl.pallas_call(
        _merge_kernel,
        grid=(D_MODEL // tn, m // TM),
        in_specs=[pl.BlockSpec((TM, k), lambda j, i: (i, 0)),
                  pl.BlockSpec((TM, k), lambda j, i: (i, 0)),
                  wspec, wspec,
                  pl.BlockSpec((TM, tn), lambda j, i: (i, COL_GA // tn + j)),
                  pl.BlockSpec((TM, tn), lambda j, i: (i, COL_GB // tn + j))],
        out_specs=pl.BlockSpec((TM, tn), lambda j, i: (i, j)),
        out_shape=jax.ShapeDtypeStruct((m, D_MODEL), BF16),
        scratch_shapes=[pltpu.VMEM((k, tn), BF16), pltpu.VMEM((k, tn), BF16)],
        compiler_params=_params("arbitrary", "arbitrary"),
        name="merge",
    )(o_a, o_b, w_a, w_b, proj, proj)


def _out_ln_kernel(mg_ref, w_ref, x_ref, g_ref, b_ref, o_ref, ob_ref, wb_ref, *, layer):
    @pl.when(pl.program_id(0) == 0)
    def _():
        wb_ref[...] = w_ref[...].astype(BF16)

    h = ALPHA * x_ref[...] + _dot(mg_ref[...], wb_ref[...])
    y = _layer_norm(h, g_ref[layer:layer + 1, :], b_ref[layer:layer + 1, :])
    o_ref[...] = y
    ob_ref[...] = y.astype(ob_ref.dtype)


def _out_ln(merged, w_out, x, g, b, layer):
    m, k = merged.shape
    row = lambda i: (i, 0)
    fix = lambda i: (0, 0)
    return pl.pallas_call(
        functools.partial(_out_ln_kernel, layer=layer),
        grid=(m // TM_LN,),
        in_specs=[pl.BlockSpec((TM_LN, k), row),
                  pl.BlockSpec((None, k, D_MODEL), lambda i: (layer, 0, 0), pipeline_mode=pl.Buffered(1)),
                  pl.BlockSpec((TM_LN, D_MODEL), row),
                  pl.BlockSpec((DEPTH, D_MODEL), fix),
                  pl.BlockSpec((DEPTH, D_MODEL), fix)],
        out_specs=[pl.BlockSpec((TM_LN, D_MODEL), row), pl.BlockSpec((TM_LN, D_MODEL), row)],
        out_shape=[jax.ShapeDtypeStruct((m, D_MODEL), F32), jax.ShapeDtypeStruct((m, D_MODEL), BF16)],
        scratch_shapes=[pltpu.VMEM((k, D_MODEL), BF16)],
        compiler_params=_params("arbitrary"),
        name="out_ln",
    )(merged, w_out, x, g, b)


def _gelu(x):
    return 0.5 * x * (1.0 + lax.erf(x * (1.0 / math.sqrt(2.0))))


def _ffn_up_kernel(x_ref, wa_ref, wv_ref, a_ref, v_ref, st_s_ref, st_p_ref, wab_ref, wvb_ref,
                   *, slot_rows, n_sample_seqs):
    i = pl.program_id(1)
    tn = wab_ref.shape[1]
    tail = FFN_CONV - 1

    @pl.when(i == 0)
    def _():
        wab_ref[...] = wa_ref[...].astype(BF16)
        wvb_ref[...] = wv_ref[...].astype(BF16)

    x = x_ref[...]
    a = _dot(x, wab_ref[...])
    a_ref[...] = a.astype(a_ref.dtype)
    v_ref[...] = _dot(x, wvb_ref[...]).astype(v_ref.dtype)

    @pl.when(i == 0)
    def _():
        a3 = a.reshape(TM // slot_rows, slot_rows, tn)
        st_s_ref[...] = a3[:n_sample_seqs, slot_rows - tail:]

    @pl.when(i > 0)
    def _():
        st_p_ref[0] = a[TM - tail:]


def _ffn_up(xb, w_up, layer, *, batch, seq, slot_rows, n_sample_seqs, tn=512):
    m, k = xb.shape
    nj = D_FF // tn
    tail = FFN_CONV - 1
    tps = seq // TM
    return pl.pallas_call(
        functools.partial(_ffn_up_kernel, slot_rows=slot_rows, n_sample_seqs=n_sample_seqs),
        grid=(nj, m // TM),
        in_specs=[pl.BlockSpec((TM, k), lambda j, i: (i, 0)),
                  pl.BlockSpec((None, k, tn), lambda j, i: (layer, 0, j)),
                  pl.BlockSpec((None, k, tn), lambda j, i: (layer, 0, nj + j))],
        out_specs=[pl.BlockSpec((TM, tn), lambda j, i: (i, j)),
                   pl.BlockSpec((TM, tn), lambda j, i: (i, j)),
                   pl.BlockSpec((n_sample_seqs, tail, tn), lambda j, i: (0, 0, j)),
                   pl.BlockSpec((1, tail, tn), lambda j, i: (jnp.maximum(i - 1, 0) // tps, 0, j))],
        out_shape=[jax.ShapeDtypeStruct((m, D_FF), BF16),
                   jax.ShapeDtypeStruct((m, D_FF), BF16),
                   jax.ShapeDtypeStruct((n_sample_seqs, tail, D_FF), F32),
                   jax.ShapeDtypeStruct((batch, tail, D_FF), F32)],
        scratch_shapes=[pltpu.VMEM((k, tn), BF16), pltpu.VMEM((k, tn), BF16)],
        compiler_params=_params("arbitrary", "arbitrary"),
        name="ffn_up",
    )(xb, w_up, w_up)


def _gated_down(a_ref, v_ref, prev, cw_ref, cb_ref, w_ref, layer, seqs, seq_rows):
    n = seq_rows
    acc = None
    for c0 in range(0, D_FF, FFN_KCHUNK):
        cs = slice(c0, c0 + FFN_KCHUNK)
        a = a_ref[:, cs].astype(F32).reshape(seqs, n, FFN_KCHUNK)
        v = v_ref[:, cs].astype(F32).reshape(seqs, n, FFN_KCHUNK)
        ext = jnp.concatenate([prev[:, :, cs], a], axis=1)
        conv = (cb_ref[layer:layer + 1, cs] + cw_ref[2:3, cs] * a + cw_ref[1:2, cs] * ext[:, 7:7 + n]
                + cw_ref[0:1, cs] * ext[:, 6:6 + n])
        hid = (_gelu(conv) * v).reshape(seqs * n, FFN_KCHUNK).astype(BF16)
        part = _dot(hid, w_ref[cs, :])
        acc = part if acc is None else acc + part
    return acc


def _ffn_tail_kernel(a_ref, v_ref, st_ref, cw_ref, cb_ref, w_ref, x_ref, g_ref, b_ref, *out_refs,
                     layer, slot_rows, n_sample_seqs):
    i = pl.program_id(0)
    tail = FFN_CONV - 1

    def finish(d):
        y = _layer_norm(ALPHA * x_ref[...] + d, g_ref[layer:layer + 1, :], b_ref[layer:layer + 1, :])
        out_refs[0][...] = y
        if len(out_refs) > 1:
            out_refs[1][...] = y.astype(out_refs[1].dtype)

    @pl.when(i == 0)
    def _():
        prev = jnp.concatenate([jnp.zeros((n_sample_seqs, SUBLANES - tail, D_FF), F32), st_ref[...]], axis=1)
        finish(_gated_down(a_ref, v_ref, prev, cw_ref, cb_ref, w_ref, layer, n_sample_seqs, slot_rows))

    @pl.when(i == 1)
    def _():
        prev = jnp.zeros((1, SUBLANES, D_FF), F32)
        finish(_gated_down(a_ref, v_ref, prev, cw_ref, cb_ref, w_ref, layer, 1, TM_LN))


def _ffn_main_kernel(a_ref, v_ref, meta_ref, cw_ref, cb_ref, w_ref, x_ref, g_ref, b_ref, *rest,
                     layer, tiles_per_seq, n_out):
    out_refs, prev_ref = rest[len(rest) - 1 - n_out:-1], rest[-1]
    i = pl.program_id(0)

    @pl.when(i == 0)
    def _():
        prev_ref[...] = jnp.zeros(prev_ref.shape, F32)

    first = i % tiles_per_seq == 0
    prev = jnp.where(first, meta_ref[N_META - SUBLANES:, :].astype(F32), prev_ref[...])
    d = _gated_down(a_ref, v_ref, prev[None], cw_ref, cb_ref, w_ref, layer, 1, TM_LN)
    prev_ref[...] = a_ref[TM_LN - SUBLANES:, :].astype(F32)
    y = _layer_norm(ALPHA * x_ref[...] + d, g_ref[layer:layer + 1, :], b_ref[layer:layer + 1, :])
    out_refs[0][...] = y
    if n_out > 1:
        out_refs[1][...] = y.astype(out_refs[1].dtype)


def _ffn_down(a_b, v_b, conv_w, conv_b, st_sample, w_down_b, x, g, b, layer, *, final, seq, slot_rows,
              row_meta, row_prompt):
    m, k = a_b.shape
    n_s = st_sample.shape[1]
    t0 = row_prompt // TM_LN
    assert n_s * slot_rows == TM_LN and row_prompt == 2 * TM_LN and row_meta == TM_LN and seq % TM_LN == 0
    fix = lambda i: (0, 0)
    shared = [pl.BlockSpec((None, FFN_CONV, k), lambda i: (layer, 0, 0)),
              pl.BlockSpec((DEPTH, k), fix),
              pl.BlockSpec((None, k, D_MODEL), lambda i: (layer, 0, 0), pipeline_mode=pl.Buffered(1))]
    norm = [pl.BlockSpec((DEPTH, D_MODEL), fix), pl.BlockSpec((DEPTH, D_MODEL), fix)]
    slab = [jax.ShapeDtypeStruct((m, D_MODEL), F32), jax.ShapeDtypeStruct((m, D_MODEL), BF16)]
    row = lambda i: (i, 0)
    tail_out = ([pl.BlockSpec((TM_LN, D_MODEL), row)] if final
                else [pl.BlockSpec((TM_LN, D_MODEL), row), pl.BlockSpec((TM_LN, D_MODEL), row)])
    tail = pl.pallas_call(
        functools.partial(_ffn_tail_kernel, layer=layer, slot_rows=slot_rows, n_sample_seqs=n_s),
        grid=(1 if final else 2,),
        in_specs=[pl.BlockSpec((TM_LN, k), row), pl.BlockSpec((TM_LN, k), row),
                  pl.BlockSpec((None, n_s, FFN_CONV - 1, k), lambda i: (layer, 0, 0, 0))] + shared
                 + [pl.BlockSpec((TM_LN, D_MODEL), row)] + norm,
        out_specs=tail_out,
        out_shape=[jax.ShapeDtypeStruct((TM_LN, D_MODEL), F32)] if final else slab,
        compiler_params=_params("arbitrary"),
        name="ffn_tail",
    )(a_b, v_b, st_sample, conv_w, conv_b, w_down_b, x, g, b)

    prow = lambda i: (t0 + i, 0)
    main_in = ([pl.BlockSpec((TM_LN, k), prow), pl.BlockSpec((TM_LN, k), prow),
                pl.BlockSpec((N_META, k), lambda i: (row_meta // N_META, 0))] + shared
               + [pl.BlockSpec((TM_LN, D_MODEL), prow)] + norm)
    main = functools.partial(_ffn_main_kernel, layer=layer, tiles_per_seq=seq // TM_LN)
    operands = (a_b, v_b, a_b, conv_w, conv_b, w_down_b, x, g, b)
    if final:
        y_prompt = pl.pallas_call(
            functools.partial(main, n_out=1),
            grid=((m - row_prompt) // TM_LN,),
            in_specs=main_in,
            out_specs=[pl.BlockSpec((TM_LN, D_MODEL), row)],
            out_shape=[jax.ShapeDtypeStruct((m - row_prompt, D_MODEL), F32)],
            scratch_shapes=[pltpu.VMEM((SUBLANES, k), F32)],
            compiler_params=_params("arbitrary"),
            name="ffn_main_final",
        )(*operands)[0]
        return tail[0], y_prompt
    n_in = len(operands)
    return pl.pallas_call(
        functools.partial(main, n_out=2),
        grid=((m - row_prompt) // TM_LN,),
        in_specs=main_in + [pl.BlockSpec(memory_space=pl.ANY), pl.BlockSpec(memory_space=pl.ANY)],
        out_specs=[pl.BlockSpec((TM_LN, D_MODEL), prow), pl.BlockSpec((TM_LN, D_MODEL), prow)],
        out_shape=slab,
        scratch_shapes=[pltpu.VMEM((SUBLANES, k), F32)],
        input_output_aliases={n_in: 0, n_in + 1: 1},
        compiler_params=_params("arbitrary"),
        name="ffn_main",
    )(*operands, *tail)


def _constants():
    t = jnp.arange(CHUNK)[:, None]
    s = jnp.arange(CHUNK)[None, :]
    level = jnp.full((CHUNK, CHUNK), -1, jnp.int32)
    for lev in range(N_LEVELS):
        level = jnp.where((t > s) & (((t ^ s) >> lev) == 1), lev, level)
    level = jnp.where(t == s, N_LEVELS, level)
    return dict(tri=(t >= s).astype(BF16), level=level)


def _head_lanes(v):
    return jnp.pad(v.astype(F32), ((0, 0), (0, LANES - SSM_HEADS)))


def kernel(x_prompt, x_sample, state_hgrn, state_ssm, state_ssm_conv, state_ffn_conv, meta_tokens, w_in,
           hgrn_lb_logits, hgrn_norm_w, w_proj_a, ssm_conv_w, ssm_conv_b, ssm_dt_bias, ssm_a_log, ssm_d,
           ssm_norm_w, w_proj_b, w_out, ln1_g, ln1_b, ffn_w_up, ffn_conv_w, ffn_conv_b, ffn_w_down, ln2_g, ln2_b):
    batch, seq, _ = x_prompt.shape
    dec_batch, dec_seq, _ = x_sample.shape
    n_sample = dec_batch * dec_seq
    row_meta = n_sample
    row_prompt = TM
    assert seq % TM == 0 and dec_seq >= N_META and row_meta + dec_seq <= TM and row_meta % dec_seq == 0
    m = row_prompt + batch * seq

    x = jnp.concatenate([x_sample.reshape(n_sample, D_MODEL).astype(F32), meta_tokens.astype(F32),
                         jnp.zeros((row_prompt - n_sample - N_META, D_MODEL), F32),
                         x_prompt.reshape(batch * seq, D_MODEL).astype(F32)], axis=0)
    xb = x.astype(BF16)
    consts = _constants()
    lb_logits = hgrn_lb_logits.astype(F32)
    ssm_p = dict(ssm_conv_w=ssm_conv_w, ssm_conv_b=ssm_conv_b, a_log=_head_lanes(ssm_a_log),
                 d_skip=_head_lanes(ssm_d), ssm_norm_w=ssm_norm_w)
    dt_bias = _head_lanes(ssm_dt_bias)
    w_down_b = _cast_bf16(ffn_w_down)
    groups = (dict(batch=1, seq=N_META, row0=row_meta), dict(batch=batch, seq=seq, row0=row_prompt),
              dict(batch=dec_batch, seq=dec_seq, row0=0))

    w_in_t = jnp.swapaxes(w_in, 1, 2)
    o_a = jnp.zeros((m, HG_HEADS * HG_D), BF16)
    o_b = jnp.zeros((m, SSM_INNER), BF16)

    hg_p, ssm_pp, conv_p, ffn_p, hg_s, ssm_s, conv_s, ffn_s = ([] for _ in range(8))
    y_sample = y_prompt = None
    for l in range(DEPTH):
        proj = _in_proj(xb, w_in_t, l)
        dt = _dt_proj(xb, w_in_t, dt_bias, l)

        o_a, hg_m = _gla(proj, o_a, lb_logits, hgrn_norm_w, jnp.zeros((1, HG_HEADS, HG_D, HG_D), F32), consts,
                         layer=l, **groups[0])
        o_b, ssm_m, conv_m = _ssd(proj, dt, o_b, ssm_p, jnp.zeros((1, SSM_HEADS, SSM_P, SSM_N), F32),
                                  jnp.zeros((1, SSM_CONV - 1, SSM_XBC), F32), consts, layer=l, **groups[0])
        seed = lambda s: jnp.broadcast_to(s, (batch,) + s.shape[1:])
        o_a, s_hg = _gla(proj, o_a, lb_logits, hgrn_norm_w, seed(hg_m), consts, layer=l, **groups[1])
        o_b, s_ssm, s_conv = _ssd(proj, dt, o_b, ssm_p, seed(ssm_m), seed(conv_m), consts, layer=l, **groups[1])
        hg_p.append(s_hg)
        ssm_pp.append(s_ssm)
        conv_p.append(s_conv)
        o_a, s_hg = _gla(proj, o_a, lb_logits, hgrn_norm_w, state_hgrn[l], consts, layer=l, **groups[2])
        o_b, s_ssm, s_conv = _ssd(proj, dt, o_b, ssm_p, state_ssm[l], state_ssm_conv[l], consts, layer=l,
                                  **groups[2])
        hg_s.append(s_hg)
        ssm_s.append(s_ssm)
        conv_s.append(s_conv)

        merged = _merge(o_a, o_b, w_proj_a, w_proj_b, proj, l)
        x, xb = _out_ln(merged, w_out, x, ln1_g, ln1_b, l)
        a_b, v_b, f_s, f_p = _ffn_up(xb, ffn_w_up, l, batch=batch, seq=seq, slot_rows=dec_seq,
                                     n_sample_seqs=dec_batch)
        ffn_s.append(f_s)
        ffn_p.append(f_p)
        down = _ffn_down(a_b, v_b, ffn_conv_w, ffn_conv_b, state_ffn_conv, w_down_b, x, ln2_g, ln2_b, l,
                         final=l == DEPTH - 1, seq=seq, slot_rows=dec_seq, row_meta=row_meta,
                         row_prompt=row_prompt)
        if l < DEPTH - 1:
            x, xb = down
        else:
            y_sample, y_prompt = down
    return (y_prompt.reshape(batch, seq, D_MODEL), y_sample.reshape(dec_batch, dec_seq, D_MODEL),
            jnp.stack(hg_p), jnp.stack(ssm_pp), jnp.stack(conv_p), jnp.stack(ffn_p),
            jnp.stack(hg_s), jnp.stack(ssm_s), jnp.stack(conv_s), jnp.stack(ffn_s))
```

```python
import functools
import math

import jax
import jax.numpy as jnp
from jax import lax
from jax.experimental import pallas as pl
from jax.experimental.pallas import tpu as pltpu

F32 = jnp.float32
BF16 = jnp.bfloat16

D_MODEL = 2048
DEPTH = 2
N_META = 16
HG_HEADS = 16
HG_D = 128
SSM_HEADS = 32
SSM_P = 64
SSM_GROUPS = 4
SSM_HPG = SSM_HEADS // SSM_GROUPS
SSM_N = 128
SSM_INNER = SSM_HEADS * SSM_P
SSM_GW = SSM_HPG * SSM_P
SSM_BC = 2 * SSM_GROUPS * SSM_N
SSM_CONV = 4
SSM_XBC = SSM_INNER + SSM_BC
D_FF = 5632
FFN_CONV = 3
ALPHA = (2.0 * DEPTH) ** 0.25
LN_EPS = 1e-5
RMS_EPS = 1e-6

LANES = 128
SUBLANES = 8
CHUNK = 128
N_LEVELS = int(math.log2(CHUNK))
TM = 512
TM_LN = 256
TM_PROJ_MAX = 1152
VMEM_LIMIT = 48 * 1024 * 1024
NEG_BIG = -1e30
LOG2_E = 1.4426950408889634

COL_Q, COL_F, COL_I, COL_G, COL_Z = 0, 2048, 4096, 6144, 8192
COL_XS, COL_BC = 10240, 12288
N_MAIN = 13312
COL_DT = 13312
COL_GATES = COL_DT + SSM_HEADS
COL_GA, COL_GB = N_MAIN, N_MAIN + D_MODEL


def _params(*semantics):
    return pltpu.CompilerParams(dimension_semantics=semantics, vmem_limit_bytes=VMEM_LIMIT)


def _nt(a, b):
    return lax.dot_general(a, b, (((1,), (1,)), ((), ())), preferred_element_type=F32)


def _dot(a, b):
    return jnp.dot(a, b, preferred_element_type=F32)


def _cumsum_rows(tri_b, g):
    n = g.shape[1]
    hi = g.astype(BF16)
    lo = (g - hi.astype(F32)).astype(BF16)
    s = _dot(tri_b, jnp.concatenate([hi, lo], axis=1))
    return s[:, :n] + s[:, n:]


def _silu(x):
    return x * jax.nn.sigmoid(x)


def _layer_norm(h, g, b):
    mu = jnp.mean(h, axis=-1, keepdims=True)
    d = h - mu
    var = jnp.mean(d * d, axis=-1, keepdims=True)
    return d * lax.rsqrt(var + LN_EPS) * g + b


def _pad_rows(x, rows):
    if x.shape[0] == rows:
        return x
    return jnp.concatenate([x, jnp.zeros((rows - x.shape[0],) + x.shape[1:], x.dtype)], axis=0)


def _lane_pair(row, h0):
    lane = lax.broadcasted_iota(jnp.int32, (1, LANES), 1)
    return jnp.where(lane < SSM_P, jnp.broadcast_to(row[:, h0:h0 + 1], (1, LANES)),
                     jnp.broadcast_to(row[:, h0 + 1:h0 + 2], (1, LANES)))


def _cast_kernel(x_ref, o_ref):
    o_ref[...] = x_ref[...].astype(o_ref.dtype)


def _cast_bf16(w, rows=512):
    lead, r, c = w.shape
    return pl.pallas_call(
        _cast_kernel,
        grid=(lead, r // rows),
        in_specs=[pl.BlockSpec((None, rows, c), lambda l, i: (l, i, 0))],
        out_specs=pl.BlockSpec((None, rows, c), lambda l, i: (l, i, 0)),
        out_shape=jax.ShapeDtypeStruct(w.shape, BF16),
        compiler_params=_params("arbitrary", "arbitrary"),
        name="cast_bf16",
    )(w)


def _slab_kernel(xs_ref, meta_ref, xp_ref, o_ref, ob_ref):
    i = pl.program_id(0)

    @pl.when(i == 0)
    def _():
        n_s, n_m = xs_ref.shape[0], meta_ref.shape[0]
        head = jnp.concatenate([xs_ref[...], meta_ref[...], jnp.zeros((TM - n_s - n_m, D_MODEL), F32)], axis=0)
        o_ref[...] = head
        ob_ref[...] = head.astype(ob_ref.dtype)

    @pl.when(i > 0)
    def _():
        o_ref[...] = xp_ref[...]
        ob_ref[...] = xp_ref[...].astype(ob_ref.dtype)


def _token_slab(x_sample, meta, x_prompt):
    m = TM + x_prompt.shape[0]
    fix = lambda i: (0, 0)
    return pl.pallas_call(
        _slab_kernel,
        grid=(m // TM,),
        in_specs=[pl.BlockSpec(x_sample.shape, fix), pl.BlockSpec(meta.shape, fix),
                  pl.BlockSpec((TM, D_MODEL), lambda i: (jnp.maximum(i - 1, 0), 0))],
        out_specs=[pl.BlockSpec((TM, D_MODEL), lambda i: (i, 0)), pl.BlockSpec((TM, D_MODEL), lambda i: (i, 0))],
        out_shape=[jax.ShapeDtypeStruct((m, D_MODEL), F32), jax.ShapeDtypeStruct((m, D_MODEL), BF16)],
        compiler_params=_params("arbitrary"),
        name="token_slab",
    )(x_sample, meta, x_prompt)


def _in_proj_kernel(x_ref, w_ref, o_ref, wb_ref):
    @pl.when(pl.program_id(1) == 0)
    def _():
        wb_ref[...] = w_ref[0].T.astype(BF16)

    o_ref[...] = _dot(x_ref[...], wb_ref[...])


def _row_tile(m, cap):
    return max(t for t in range(16, cap + 1, 16) if m % t == 0)


def _in_proj(xb, w_in_t, layer, tn=1024):
    m, k = xb.shape
    tm = _row_tile(m, TM_PROJ_MAX)
    n_main = N_MAIN // tn
    n_tiles = n_main + 2 * D_MODEL // tn
    assert COL_GATES % SUBLANES == 0 and tn % SUBLANES == 0

    def window(j, i):
        row8 = jnp.where(j < n_main, j * (tn // SUBLANES), COL_GATES // SUBLANES + (j - n_main) * (tn // SUBLANES))
        return (layer, SUBLANES * row8, 0)

    return pl.pallas_call(
        _in_proj_kernel,
        grid=(n_tiles, m // tm),
        in_specs=[pl.BlockSpec((tm, k), lambda j, i: (i, 0)),
                  pl.BlockSpec((pl.Element(1), pl.Element(tn), pl.Element(k)), window)],
        out_specs=pl.BlockSpec((tm, tn), lambda j, i: (i, j)),
        out_shape=jax.ShapeDtypeStruct((m, n_tiles * tn), F32),
        scratch_shapes=[pltpu.VMEM((k, tn), BF16)],
        compiler_params=_params("arbitrary", "arbitrary"),
        name="in_proj",
    )(xb, w_in_t)


def _dt_kernel(x_ref, w_ref, b_ref, o_ref, wb_ref, *, layer):
    @pl.when(pl.program_id(0) == 0)
    def _():
        wb_ref[...] = w_ref[...].T.astype(BF16)

    raw = _dot(x_ref[...], wb_ref[...]) + b_ref[layer:layer + 1, :]
    o_ref[...] = jnp.maximum(raw, 0.0) + jnp.log1p(jnp.exp(-jnp.abs(raw)))


def _dt_proj(xb, w_in_t, dt_bias, layer):
    m, k = xb.shape
    return pl.pallas_call(
        functools.partial(_dt_kernel, layer=layer),
        grid=(m // TM,),
        in_specs=[pl.BlockSpec((TM, k), lambda i: (i, 0)),
                  pl.BlockSpec((None, LANES, k), lambda i: (layer, COL_DT // LANES, 0)),
                  pl.BlockSpec((DEPTH, LANES), lambda i: (0, 0))],
        out_specs=pl.BlockSpec((TM, LANES), lambda i: (i, 0)),
        out_shape=jax.ShapeDtypeStruct((m, LANES), F32),
        scratch_shapes=[pltpu.VMEM((k, LANES), BF16)],
        compiler_params=_params("arbitrary"),
        name="dt_proj",
    )(xb, w_in_t, dt_bias)


def _level_boundary(cum, h):
    c = cum.shape[0]
    if h >= SUBLANES:
        blk = cum.reshape(c // (2 * h), 2 * h, LANES)
        return jnp.broadcast_to(blk[:, h - 1:h, :], blk.shape).reshape(c, LANES)
    tiles = cum.reshape(c // SUBLANES, SUBLANES, LANES)
    sub = lax.broadcasted_iota(jnp.int32, tiles.shape, 1)
    out = None
    for start in range(0, SUBLANES, 2 * h):
        row = jnp.broadcast_to(tiles[:, start + h - 1:start + h, :], tiles.shape)
        out = row if out is None else jnp.where(sub >= start, row, out)
    return out.reshape(c, LANES)


def _gla_kernel(q_ref, f_ref, i_ref, g_ref, lbl_ref, nw_ref, s0_ref, tri_ref, lvl_ref, buf_ref,
                o_ref, s_out_ref, st_ref, a_ref, *, layer, heads, rows, seqs):
    del buf_ref
    c = pl.program_id(2)
    last = pl.num_programs(2) - 1
    seg = CHUNK // seqs
    n_levels = int(math.log2(min(seg, rows)))

    @pl.when(c == 0)
    def _():
        for s in range(seqs):
            for h in range(heads):
                st_ref[s * heads + h] = s0_ref[s, h].T

    logits = lbl_ref[...]
    ex = jnp.exp(logits - jnp.max(logits, axis=0, keepdims=True))
    sm = ex / jnp.sum(ex, axis=0, keepdims=True)
    lb_all = jnp.zeros((1, sm.shape[1]), F32)
    for l in range(1, layer + 1):
        lb_all = lb_all + sm[l:l + 1]

    tri = tri_ref[...]
    lvl = lvl_ref[...]
    valid = lax.broadcasted_iota(jnp.int32, (CHUNK, LANES), 0) < rows

    head_cols = [slice(h * HG_D, (h + 1) * HG_D) for h in range(heads)]
    qs, ks, vs, decays, cums, o_inters = [], [], [], [], [], []
    for h, cols in enumerate(head_cols):
        lb = lb_all[:, cols]
        q = _silu(_pad_rows(q_ref[:, cols], CHUNK))
        f = lb + (1.0 - lb) * jax.nn.sigmoid(_pad_rows(f_ref[:, cols], CHUNK))
        decay = f if rows == CHUNK else jnp.where(valid, f, 1.0)
        qs.append(q)
        ks.append(1.0 - f)
        decays.append(decay)
        cums.append(_cumsum_rows(tri, jnp.log2(decay)))

    stream_of_lane = lax.broadcasted_iota(jnp.int32, (HG_D, CHUNK), 1) // seg
    for h, cols in enumerate(head_cols):
        q, k, cum = qs[h], ks[h], cums[h]
        v32 = _pad_rows(i_ref[:, cols], CHUNK)
        vs.append(v32.astype(BF16))
        q_dec = (q * jnp.exp2(cum)).astype(BF16)
        v_t = v32.T
        if seqs == 1:
            st = st_ref[h]
            o_inters.append(_nt(q_dec, st.astype(BF16)))
            c_last = cum[CHUNK - 1:CHUNK, :]
            k_dec = (k * jnp.exp2(c_last - cum)).astype(BF16)
            st_ref[h] = st * jnp.exp2(c_last) + _dot(v_t.astype(BF16), k_dec)
        else:
            ends = cum.reshape(seqs, seg, HG_D)[:, seg - 1:seg, :]
            c_last = jnp.broadcast_to(ends, (seqs, seg, HG_D)).reshape(CHUNK, HG_D)
            k_dec = (k * jnp.exp2(c_last - cum)).astype(BF16)
            parts = []
            for s in range(seqs):
                st = st_ref[s * heads + h]
                parts.append(_nt(q_dec[s * seg:(s + 1) * seg], st.astype(BF16)))
                v_s = jnp.where(stream_of_lane == s, v_t, 0.0).astype(BF16)
                st_ref[s * heads + h] = st * jnp.exp2(ends[s]) + _dot(v_s, k_dec)
            o_inters.append(jnp.concatenate(parts, axis=0))

    kbs = [k.astype(BF16) for k in ks]
    for h in range(heads):
        a_ref[h] = jnp.where(lvl == N_LEVELS, _nt(qs[h].astype(BF16), kbs[h]), 0.0)
    for h in range(heads):
        pltpu.store(a_ref.at[h], _nt((qs[h] * decays[h]).astype(BF16), kbs[h]), mask=lvl == 0)
    for lev in range(1, n_levels):
        half = 1 << lev
        for h in range(heads):
            q, k, cum = qs[h], ks[h], cums[h]
            if half < SUBLANES:
                z = jnp.exp2(-jnp.abs(cum - _level_boundary(cum, half)))
                pltpu.store(a_ref.at[h], _nt((q * z).astype(BF16), (k * z).astype(BF16)), mask=lvl == lev)
            else:
                q_parts, k_parts, uppers = [], [], []
                for start in range(0, CHUNK, 2 * half):
                    lo, up = slice(start, start + half), slice(start + half, start + 2 * half)
                    mid = cum[start + half - 1:start + half, :]
                    q_parts.append(q[up] * jnp.exp2(cum[up] - mid))
                    k_parts += [k[lo] * jnp.exp2(mid - cum[lo]), k[up]]
                    uppers.append(up)
                p = _nt(jnp.concatenate(q_parts, axis=0).astype(BF16),
                        jnp.concatenate(k_parts, axis=0).astype(BF16))
                for n, up in enumerate(uppers):
                    pltpu.store(a_ref.at[h, up, :], p[n * half:(n + 1) * half, :], mask=lvl[up, :] == lev)

    for h, cols in enumerate(head_cols):
        o = (_dot(a_ref[h].astype(BF16), vs[h]) + o_inters[h])[:rows]
        o = o * lax.rsqrt(jnp.mean(o * o, axis=-1, keepdims=True) + RMS_EPS) * nw_ref[layer:layer + 1, cols]
        o_ref[:, cols] = (o * _silu(g_ref[:, cols])).astype(o_ref.dtype)

    @pl.when(c == last)
    def _():
        for s in range(seqs):
            for h in range(heads):
                s_out_ref[s, h] = st_ref[s * heads + h].T


def _gla(proj, buf, lb_logits, norm_w, s0, consts, *, layer, batch, seq, row0, heads_per_step=16):
    seqs = CHUNK // seq if seq < CHUNK and CHUNK % seq == 0 and batch % (CHUNK // seq) == 0 else 1
    rows = CHUNK if seqs > 1 else min(seq, CHUNK)
    nc = max(seq // CHUNK, 1)
    blk0 = row0 // rows
    hp = heads_per_step
    w = hp * HG_D
    tri = consts["tri"] if seqs == 1 else consts["tri_streams"][seqs]

    def col(base):
        return lambda b, hb, c: (blk0 + b * nc + c, base // w + hb)

    par = lambda b, hb, c: (0, hb)
    state = lambda b, hb, c: (b, hb, 0, 0)
    out_rows = lambda b, hb, c: (blk0 + b * nc + c, hb)
    return pl.pallas_call(
        functools.partial(_gla_kernel, layer=layer, heads=hp, rows=rows, seqs=seqs),
        grid=(batch // seqs, HG_HEADS // hp, nc),
        in_specs=[pl.BlockSpec((rows, w), col(COL_Q)),
                  pl.BlockSpec((rows, w), col(COL_F)),
                  pl.BlockSpec((rows, w), col(COL_I)),
                  pl.BlockSpec((rows, w), col(COL_G)),
                  pl.BlockSpec((DEPTH, w), par),
                  pl.BlockSpec((DEPTH, w), par),
                  pl.BlockSpec((seqs, hp, HG_D, HG_D), state),
                  pl.BlockSpec((CHUNK, CHUNK), lambda b, hb, c: (0, 0)),
                  pl.BlockSpec((CHUNK, CHUNK), lambda b, hb, c: (0, 0)),
                  pl.BlockSpec((rows, w), out_rows)],
        out_specs=[pl.BlockSpec((rows, w), out_rows),
                   pl.BlockSpec((seqs, hp, HG_D, HG_D), state)],
        out_shape=[jax.ShapeDtypeStruct(buf.shape, buf.dtype),
                   jax.ShapeDtypeStruct((batch, HG_HEADS, HG_D, HG_D), F32)],
        scratch_shapes=[pltpu.VMEM((seqs * hp, HG_D, HG_D), F32), pltpu.VMEM((hp, CHUNK, CHUNK), F32)],
        input_output_aliases={9: 0},
        compiler_params=_params("arbitrary", "arbitrary", "arbitrary"),
        name="hgrn2",
    )(proj, proj, proj, proj, lb_logits, norm_w, s0, tri, consts["level"], buf)


def _causal_conv4(raw, ext_ref, w, bias):
    n = raw.shape[0]
    ext_ref[SUBLANES:SUBLANES + n, :] = raw
    out = (bias + w[3:4] * raw + w[2:3] * ext_ref[7:7 + n, :] + w[1:2] * ext_ref[6:6 + n, :]
           + w[0:1] * ext_ref[5:5 + n, :])
    ext_ref[0:SUBLANES, :] = raw[n - SUBLANES:]
    return out


def _ssd_kernel(xs_ref, bc_ref, z_ref, dt_ref, cw_ref, cb_ref, st_ref, alog_ref, dskip_ref, nw_ref,
                h0_ref, tri_ref, buf_ref, y_ref, h_out_ref, st_out_ref, ht_ref, px_ref, pbc_ref,
                *, layer, rows):
    del buf_ref
    c = pl.program_id(1)
    last = pl.num_programs(1) - 1
    tail = SSM_CONV - 1

    @pl.when(c == 0)
    def _():
        ht_ref[...] = h0_ref[0].T
        buf = jnp.concatenate([jnp.zeros((SUBLANES - tail, SSM_XBC), F32), st_ref[0]], axis=0)
        px_ref[0:SUBLANES, :] = buf[:, :SSM_INNER]
        pbc_ref[0:SUBLANES, :] = buf[:, SSM_INNER:]

    raw_x, raw_bc = xs_ref[...], bc_ref[...]
    st_out_ref[0, :, :SSM_INNER] = raw_x[rows - tail:]
    st_out_ref[0, :, SSM_INNER:] = raw_bc[rows - tail:]
    cw = cw_ref[...]
    cb = cb_ref[layer:layer + 1, :]
    xs = _pad_rows(_silu(_causal_conv4(raw_x, px_ref, cw[:, :SSM_INNER], cb[:, :SSM_INNER])), CHUNK)
    bc = _pad_rows(_silu(_causal_conv4(raw_bc, pbc_ref, cw[:, SSM_INNER:], cb[:, SSM_INNER:])), CHUNK)

    dt = _pad_rows(dt_ref[...], CHUNK)
    if rows < CHUNK:
        dt = jnp.where(lax.broadcasted_iota(jnp.int32, dt.shape, 0) < rows, dt, 0.0)
    a = dt * (-LOG2_E * jnp.exp(alog_ref[layer:layer + 1, :]))
    cum = _cumsum_rows(tri_ref[...], a)
    c_last = cum[CHUNK - 1:CHUNK, :]
    cum_t = cum.T
    dt_t = dt.T
    wgt_t = (jnp.exp2(c_last - cum) * dt).T
    dec = jnp.exp2(c_last)
    dskip = dskip_ref[layer:layer + 1, :]

    causal = (lax.broadcasted_iota(jnp.int32, (CHUNK, CHUNK), 0)
              >= lax.broadcasted_iota(jnp.int32, (CHUNK, CHUNK), 1))
    low = lax.broadcasted_iota(jnp.int32, (CHUNK, LANES), 1) < SSM_P
    z = z_ref[...]
    for g in range(SSM_GROUPS):
        bm = bc[:, g * SSM_N:(g + 1) * SSM_N]
        cm = bc[:, (SSM_GROUPS + g) * SSM_N:(SSM_GROUPS + g + 1) * SSM_N]
        cb_ts = _nt(cm.astype(BF16), bm.astype(BF16))
        bm_t = bm.T
        y_parts = []
        for pair in range(SSM_HPG // 2):
            h0 = g * SSM_HPG + 2 * pair
            cols = slice((h0 // 2) * LANES, (h0 // 2 + 1) * LANES)
            xp = xs[:, cols]
            ht = ht_ref[:, cols]
            lhs, rhs, upd_l, upd_r = [], [], [], []
            for h, mine in ((h0, low), (h0 + 1, jnp.logical_not(low))):
                col_b = jnp.broadcast_to(cum[:, h:h + 1], (CHUNK, CHUNK))
                lmat = jnp.exp2(jnp.where(causal, col_b - cum_t[h:h + 1, :], NEG_BIG))
                lhs.append((cb_ts * lmat * dt_t[h:h + 1, :]).astype(BF16))
                lhs.append((cm * jnp.exp2(col_b)).astype(BF16))
                xh = jnp.where(mine, xp, 0.0).astype(BF16)
                rhs.append(xh)
                rhs.append(jnp.where(mine, ht, 0.0).astype(BF16))
                upd_l.append((bm_t * wgt_t[h:h + 1, :]).astype(BF16))
                upd_r.append(xh)
            y_parts.append(_dot(jnp.concatenate(lhs, axis=1), jnp.concatenate(rhs, axis=0)))
            ht_ref[:, cols] = ht * _lane_pair(dec, h0) + _dot(jnp.concatenate(upd_l, axis=1),
                                                             jnp.concatenate(upd_r, axis=0))
            y_parts[-1] = y_parts[-1] + _lane_pair(dskip, h0) * xp
        gcols = slice(g * SSM_GW, (g + 1) * SSM_GW)
        y = jnp.concatenate(y_parts, axis=1)[:rows] * _silu(z[:, gcols])
        y = y * lax.rsqrt(jnp.mean(y * y, axis=-1, keepdims=True) + RMS_EPS) * nw_ref[layer:layer + 1, gcols]
        y_ref[:, gcols] = y.astype(y_ref.dtype)

    @pl.when(c == last)
    def _():
        h_out_ref[0] = ht_ref[...].T


def _ssd(proj, dt, buf, p, h0, conv0, consts, *, layer, batch, seq, row0):
    rows = min(seq, CHUNK)
    nc = seq // rows
    blk0 = row0 // rows
    n = SSM_N
    tail = SSM_CONV - 1
    h0 = h0.reshape(batch, SSM_INNER, n)
    row = lambda base, width: (lambda b, c: (blk0 + b * nc + c, base // width))
    par = lambda b, c: (0, 0)
    st = lambda b, c: (b, 0, 0)
    y, h_out, conv_out = pl.pallas_call(
        functools.partial(_ssd_kernel, layer=layer, rows=rows),
        grid=(batch, nc),
        in_specs=[pl.BlockSpec((rows, SSM_INNER), row(COL_XS, SSM_INNER)),
                  pl.BlockSpec((rows, SSM_BC), row(COL_BC, SSM_BC)),
                  pl.BlockSpec((rows, SSM_INNER), row(COL_Z, SSM_INNER)),
                  pl.BlockSpec((rows, LANES), row(0, LANES)),
                  pl.BlockSpec((None, SSM_CONV, SSM_XBC), lambda b, c: (layer, 0, 0)),
                  pl.BlockSpec((DEPTH, SSM_XBC), par),
                  pl.BlockSpec((1, tail, SSM_XBC), st),
                  pl.BlockSpec((DEPTH, LANES), par),
                  pl.BlockSpec((DEPTH, LANES), par),
                  pl.BlockSpec((DEPTH, SSM_INNER), par),
                  pl.BlockSpec((1, SSM_INNER, n), st),
                  pl.BlockSpec((CHUNK, CHUNK), par),
                  pl.BlockSpec((rows, SSM_INNER), row(0, SSM_INNER))],
        out_specs=[pl.BlockSpec((rows, SSM_INNER), row(0, SSM_INNER)),
                   pl.BlockSpec((1, SSM_INNER, n), st),
                   pl.BlockSpec((1, tail, SSM_XBC), st)],
        out_shape=[jax.ShapeDtypeStruct(buf.shape, buf.dtype),
                   jax.ShapeDtypeStruct((batch, SSM_INNER, n), F32),
                   jax.ShapeDtypeStruct((batch, tail, SSM_XBC), F32)],
        scratch_shapes=[pltpu.VMEM((n, SSM_INNER), F32),
                        pltpu.VMEM((SUBLANES + rows, SSM_INNER), F32),
                        pltpu.VMEM((SUBLANES + rows, SSM_BC), F32)],
        input_output_aliases={12: 0},
        compiler_params=_params("arbitrary", "arbitrary"),
        name="ssd",
    )(proj, proj, proj, dt, p["ssm_conv_w"], p["ssm_conv_b"], conv0, p["a_log"], p["d_skip"],
      p["ssm_norm_w"], h0, consts["tri"], buf)
    return y, h_out.reshape(batch, SSM_HEADS, SSM_P, n), conv_out


def _merge_kernel(oa_ref, ob_ref, wa_ref, wb_ref, ga_ref, gb_ref, o_ref, wab_ref, wbb_ref):
    @pl.when(pl.program_id(1) == 0)
    def _():
        wab_ref[...] = wa_ref[...].astype(BF16)
        wbb_ref[...] = wb_ref[...].astype(BF16)

    o_ref[...] = (jax.nn.sigmoid(ga_ref[...]) * _dot(oa_ref[...], wab_ref[...])
                  + jax.nn.sigmoid(gb_ref[...]) * _dot(ob_ref[...], wbb_ref[...])).astype(o_ref.dtype)


def _merge(o_a, o_b, w_a, w_b, proj, layer, tn=1024):
    m, k = o_a.shape
    wspec = pl.BlockSpec((None, k, tn), lambda j, i: (layer, 0, j), pipeline_mode=pl.Buffered(1))
    return pl.pallas_call(
        _merge_kernel,
        grid=(D_MODEL // tn, m // TM),
        in_specs=[pl.BlockSpec((TM, k), lambda j, i: (i, 0)),
                  pl.BlockSpec((TM, k), lambda j, i: (i, 0)),
                  wspec, wspec,
                  pl.BlockSpec((TM, tn), lambda j, i: (i, COL_GA // tn + j)),
                  pl.BlockSpec((TM, tn), lambda j, i: (i, COL_GB // tn + j))],
        out_specs=pl.BlockSpec((TM, tn), lambda j, i: (i, j)),
        out_shape=jax.ShapeDtypeStruct((m, D_MODEL), BF16),
        scratch_shapes=[pltpu.VMEM((k, tn), BF16), pltpu.VMEM((k, tn), BF16)],
        compiler_params=_params("arbitrary", "arbitrary"),
        name="merge",
    )(o_a, o_b, w_a, w_b, proj, proj)


def _out_ln_kernel(mg_ref, w_ref, x_ref, g_ref, b_ref, o_ref, ob_ref, wb_ref, *, layer):
    @pl.when(pl.program_id(0) == 0)
    def _():
        wb_ref[...] = w_ref[...].astype(BF16)

    h = ALPHA * x_ref[...] + _dot(mg_ref[...], wb_ref[...])
    y = _layer_norm(h, g_ref[layer:layer + 1, :], b_ref[layer:layer + 1, :])
    o_ref[...] = y
    ob_ref[...] = y.astype(ob_ref.dtype)


def _out_ln(merged, w_out, x, g, b, layer):
    m, k = merged.shape
    row = lambda i: (i, 0)
    fix = lambda i: (0, 0)
    return pl.pallas_call(
        functools.partial(_out_ln_kernel, layer=layer),
        grid=(m // TM_LN,),
        in_specs=[pl.BlockSpec((TM_LN, k), row),
                  pl.BlockSpec((None, k, D_MODEL), lambda i: (layer, 0, 0), pipeline_mode=pl.Buffered(1)),
                  pl.BlockSpec((TM_LN, D_MODEL), row),
                  pl.BlockSpec((DEPTH, D_MODEL), fix),
                  pl.BlockSpec((DEPTH, D_MODEL), fix)],
        out_specs=[pl.BlockSpec((TM_LN, D_MODEL), row), pl.BlockSpec((TM_LN, D_MODEL), row)],
        out_shape=[jax.ShapeDtypeStruct((m, D_MODEL), F32), jax.ShapeDtypeStruct((m, D_MODEL), BF16)],
        scratch_shapes=[pltpu.VMEM((k, D_MODEL), BF16)],
        compiler_params=_params("arbitrary"),
        name="out_ln",
    )(merged, w_out, x, g, b)


def _gelu(x):
    return 0.5 * x * (1.0 + lax.erf(x * (1.0 / math.sqrt(2.0))))


def _ffn_up_kernel(x_ref, wa_ref, wv_ref, cw_ref, cb_ref, st_ref, h_ref, st_s_ref, st_p_ref,
                   wab_ref, wvb_ref, prev_ref, meta_ref, *, layer, slot_rows, meta_slot, tiles_per_seq):
    i = pl.program_id(1)
    tn = wab_ref.shape[1]
    tail = FFN_CONV - 1

    @pl.when(i == 0)
    def _():
        wab_ref[...] = wa_ref[...].astype(BF16)
        wvb_ref[...] = wv_ref[...].astype(BF16)

    x = x_ref[...]
    a = _dot(x, wab_ref[...])
    w = cw_ref[...]
    bias = cb_ref[layer:layer + 1, :]

    def gated(conv):
        act = _gelu(conv).reshape(TM, tn)
        return (act * _dot(x, wvb_ref[...])).astype(h_ref.dtype)

    @pl.when(i == 0)
    def _():
        slots = TM // slot_rows
        n_s = st_ref.shape[0]
        a3 = a.reshape(slots, slot_rows, tn)
        prev = jnp.concatenate(
            [jnp.concatenate([jnp.zeros((n_s, SUBLANES - tail, tn), F32), st_ref[...]], axis=1),
             jnp.zeros((slots - n_s, SUBLANES, tn), F32)], axis=0)
        ext = jnp.concatenate([prev, a3], axis=1)
        n = slot_rows
        conv = bias + w[2:3] * a3 + w[1:2] * ext[:, 7:7 + n] + w[0:1] * ext[:, 6:6 + n]
        st_s_ref[...] = a3[:n_s, n - tail:]
        meta_ref[...] = a3[meta_slot, N_META - tail:N_META]
        h_ref[...] = gated(conv)

    @pl.when(i > 0)
    def _():
        @pl.when((i - 1) % tiles_per_seq == 0)
        def _():
            prev_ref[...] = jnp.concatenate([jnp.zeros((SUBLANES - tail, tn), F32), meta_ref[...]], axis=0)

        ext = jnp.concatenate([prev_ref[...], a], axis=0)
        conv = bias + w[2:3] * a + w[1:2] * ext[7:7 + TM] + w[0:1] * ext[6:6 + TM]
        prev_ref[...] = a[TM - SUBLANES:]
        st_p_ref[0] = a[TM - tail:]
        h_ref[...] = gated(conv)


def _ffn_up(xb, w_up, conv_w, conv_b, st_sample, layer, *, batch, seq, slot_rows, meta_slot, tn=512):
    m, k = xb.shape
    nj = D_FF // tn
    tail = FFN_CONV - 1
    tps = seq // TM
    n_s = st_sample.shape[1]
    return pl.pallas_call(
        functools.partial(_ffn_up_kernel, layer=layer, slot_rows=slot_rows, meta_slot=meta_slot,
                          tiles_per_seq=tps),
        grid=(nj, m // TM),
        in_specs=[pl.BlockSpec((TM, k), lambda j, i: (i, 0)),
                  pl.BlockSpec((None, k, tn), lambda j, i: (layer, 0, j)),
                  pl.BlockSpec((None, k, tn), lambda j, i: (layer, 0, nj + j)),
                  pl.BlockSpec((None, FFN_CONV, tn), lambda j, i: (layer, 0, j)),
                  pl.BlockSpec((DEPTH, tn), lambda j, i: (0, j)),
                  pl.BlockSpec((None, n_s, tail, tn), lambda j, i: (layer, 0, 0, j))],
        out_specs=[pl.BlockSpec((TM, tn), lambda j, i: (i, j)),
                   pl.BlockSpec((n_s, tail, tn), lambda j, i: (0, 0, j)),
                   pl.BlockSpec((1, tail, tn), lambda j, i: (jnp.maximum(i - 1, 0) // tps, 0, j))],
        out_shape=[jax.ShapeDtypeStruct((m, D_FF), BF16),
                   jax.ShapeDtypeStruct((n_s, tail, D_FF), F32),
                   jax.ShapeDtypeStruct((batch, tail, D_FF), F32)],
        scratch_shapes=[pltpu.VMEM((k, tn), BF16), pltpu.VMEM((k, tn), BF16),
                        pltpu.VMEM((SUBLANES, tn), F32), pltpu.VMEM((tail, tn), F32)],
        compiler_params=_params("arbitrary", "arbitrary"),
        name="ffn_up",
    )(xb, w_up, w_up, conv_w, conv_b, st_sample)


def _ffn_down_kernel(h_ref, w_ref, x_ref, g_ref, b_ref, o_ref, ob_ref, *, layer):
    y = _layer_norm(ALPHA * x_ref[...] + _dot(h_ref[...], w_ref[...]),
                    g_ref[layer:layer + 1, :], b_ref[layer:layer + 1, :])
    o_ref[...] = y
    ob_ref[...] = y.astype(ob_ref.dtype)


def _ffn_down_final_kernel(h_ref, w_ref, x_ref, g_ref, b_ref, ys_ref, yp_ref, *, layer, prompt_tile0):
    i = pl.program_id(0)
    y = _layer_norm(ALPHA * x_ref[...] + _dot(h_ref[...], w_ref[...]),
                    g_ref[layer:layer + 1, :], b_ref[layer:layer + 1, :])

    @pl.when(i == 0)
    def _():
        ys_ref[...] = y

    @pl.when(i >= prompt_tile0)
    def _():
        yp_ref[...] = y


def _ffn_down(h, w_down_b, x, g, b, layer, *, final, n_sample, row_prompt):
    m, k = h.shape
    row = lambda i: (i, 0)
    fix = lambda i: (0, 0)
    in_specs = [pl.BlockSpec((TM_LN, k), row),
                pl.BlockSpec((None, k, D_MODEL), lambda i: (layer, 0, 0), pipeline_mode=pl.Buffered(1)),
                pl.BlockSpec((TM_LN, D_MODEL), row),
                pl.BlockSpec((DEPTH, D_MODEL), fix),
                pl.BlockSpec((DEPTH, D_MODEL), fix)]
    if not final:
        return pl.pallas_call(
            functools.partial(_ffn_down_kernel, layer=layer),
            grid=(m // TM_LN,),
            in_specs=in_specs,
            out_specs=[pl.BlockSpec((TM_LN, D_MODEL), row), pl.BlockSpec((TM_LN, D_MODEL), row)],
            out_shape=[jax.ShapeDtypeStruct((m, D_MODEL), F32), jax.ShapeDtypeStruct((m, D_MODEL), BF16)],
            compiler_params=_params("arbitrary"),
            name="ffn_down",
        )(h, w_down_b, x, g, b)
    assert n_sample == TM_LN and row_prompt % TM_LN == 0
    t0 = row_prompt // TM_LN
    return pl.pallas_call(
        functools.partial(_ffn_down_final_kernel, layer=layer, prompt_tile0=t0),
        grid=(m // TM_LN,),
        in_specs=in_specs,
        out_specs=[pl.BlockSpec((TM_LN, D_MODEL), fix),
                   pl.BlockSpec((TM_LN, D_MODEL), lambda i: (jnp.maximum(i - t0, 0), 0))],
        out_shape=[jax.ShapeDtypeStruct((n_sample, D_MODEL), F32),
                   jax.ShapeDtypeStruct((m - row_prompt, D_MODEL), F32)],
        compiler_params=_params("arbitrary"),
        name="ffn_down_final",
    )(h, w_down_b, x, g, b)


def _constants():
    t = jnp.arange(CHUNK)[:, None]
    s = jnp.arange(CHUNK)[None, :]
    level = jnp.full((CHUNK, CHUNK), -1, jnp.int32)
    for lev in range(N_LEVELS):
        level = jnp.where((t > s) & (((t ^ s) >> lev) == 1), lev, level)
    level = jnp.where(t == s, N_LEVELS, level)
    tri_streams = {n: ((t >= s) & (t // (CHUNK // n) == s // (CHUNK // n))).astype(BF16) for n in (2, 4, 8)}
    return dict(tri=(t >= s).astype(BF16), level=level, tri_streams=tri_streams)


def _head_lanes(v):
    return jnp.pad(v.astype(F32), ((0, 0), (0, LANES - SSM_HEADS)))


def kernel(x_prompt, x_sample, state_hgrn, state_ssm, state_ssm_conv, state_ffn_conv, meta_tokens, w_in,
           hgrn_lb_logits, hgrn_norm_w, w_proj_a, ssm_conv_w, ssm_conv_b, ssm_dt_bias, ssm_a_log, ssm_d,
           ssm_norm_w, w_proj_b, w_out, ln1_g, ln1_b, ffn_w_up, ffn_conv_w, ffn_conv_b, ffn_w_down, ln2_g, ln2_b):
    batch, seq, _ = x_prompt.shape
    dec_batch, dec_seq, _ = x_sample.shape
    n_sample = dec_batch * dec_seq
    row_meta = n_sample
    row_prompt = TM
    assert seq % TM == 0 and dec_seq >= N_META and row_meta + dec_seq <= TM and row_meta % dec_seq == 0
    m = row_prompt + batch * seq

    x, xb = _token_slab(x_sample.reshape(n_sample, D_MODEL).astype(F32), meta_tokens.astype(F32),
                        x_prompt.reshape(batch * seq, D_MODEL).astype(F32))
    consts = _constants()
    lb_logits = hgrn_lb_logits.astype(F32)
    ssm_p = dict(ssm_conv_w=ssm_conv_w, ssm_conv_b=ssm_conv_b, a_log=_head_lanes(ssm_a_log),
                 d_skip=_head_lanes(ssm_d), ssm_norm_w=ssm_norm_w)
    dt_bias = _head_lanes(ssm_dt_bias)
    w_down_b = _cast_bf16(ffn_w_down)
    groups = (dict(batch=1, seq=N_META, row0=row_meta), dict(batch=batch, seq=seq, row0=row_prompt),
              dict(batch=dec_batch, seq=dec_seq, row0=0))

    w_in_t = jnp.swapaxes(w_in, 1, 2)
    o_a = jnp.zeros((m, HG_HEADS * HG_D), BF16)
    o_b = jnp.zeros((m, SSM_INNER), BF16)

    hg_p, ssm_pp, conv_p, ffn_p, hg_s, ssm_s, conv_s, ffn_s = ([] for _ in range(8))
    y_sample = y_prompt = None
    for l in range(DEPTH):
        proj = _in_proj(xb, w_in_t, l)
        dt = _dt_proj(xb, w_in_t, dt_bias, l)

        o_a, hg_m = _gla(proj, o_a, lb_logits, hgrn_norm_w, jnp.zeros((1, HG_HEADS, HG_D, HG_D), F32), consts,
                         layer=l, **groups[0])
        o_b, ssm_m, conv_m = _ssd(proj, dt, o_b, ssm_p, jnp.zeros((1, SSM_HEADS, SSM_P, SSM_N), F32),
                                  jnp.zeros((1, SSM_CONV - 1, SSM_XBC), F32), consts, layer=l, **groups[0])
        seed = lambda s: jnp.broadcast_to(s, (batch,) + s.shape[1:])
        o_a, s_hg = _gla(proj, o_a, lb_logits, hgrn_norm_w, seed(hg_m), consts, layer=l, **groups[1])
        o_b, s_ssm, s_conv = _ssd(proj, dt, o_b, ssm_p, seed(ssm_m), seed(conv_m), consts, layer=l, **groups[1])
        hg_p.append(s_hg)
        ssm_pp.append(s_ssm)
        conv_p.append(s_conv)
        o_a, s_hg = _gla(proj, o_a, lb_logits, hgrn_norm_w, state_hgrn[l], consts, layer=l, **groups[2])
        o_b, s_ssm, s_conv = _ssd(proj, dt, o_b, ssm_p, state_ssm[l], state_ssm_conv[l], consts, layer=l,
                                  **groups[2])
        hg_s.append(s_hg)
        ssm_s.append(s_ssm)
        conv_s.append(s_conv)

        merged = _merge(o_a, o_b, w_proj_a, w_proj_b, proj, l)
        x, xb = _out_ln(merged, w_out, x, ln1_g, ln1_b, l)
        hid, f_s, f_p = _ffn_up(xb, ffn_w_up, ffn_conv_w, ffn_conv_b, state_ffn_conv, l, batch=batch, seq=seq,
                                slot_rows=dec_seq, meta_slot=row_meta // dec_seq)
        ffn_s.append(f_s)
        ffn_p.append(f_p)
        if l < DEPTH - 1:
            x, xb = _ffn_down(hid, w_down_b, x, ln2_g, ln2_b, l, final=False, n_sample=n_sample,
                              row_prompt=row_prompt)
        else:
            y_sample, y_prompt = _ffn_down(hid, w_down_b, x, ln2_g, ln2_b, l, final=True, n_sample=n_sample,
                                           row_prompt=row_prompt)
    return (y_prompt.reshape(batch, seq, D_MODEL), y_sample.reshape(dec_batch, dec_seq, D_MODEL),
            jnp.stack(hg_p), jnp.stack(ssm_pp), jnp.stack(conv_p), jnp.stack(ffn_p),
            jnp.stack(hg_s), jnp.stack(ssm_s), jnp.stack(conv_s), jnp.stack(ffn_s))
```

```python
import functools
import math

import jax
import jax.numpy as jnp
from jax import lax
from jax.experimental import pallas as pl
from jax.experimental.pallas import tpu as pltpu

F32 = jnp.float32
BF16 = jnp.bfloat16

D_MODEL = 2048
DEPTH = 2
N_META = 16
HG_HEADS = 16
HG_D = 128
SSM_HEADS = 32
SSM_P = 64
SSM_GROUPS = 4
SSM_HPG = SSM_HEADS // SSM_GROUPS
SSM_N = 128
SSM_INNER = SSM_HEADS * SSM_P
SSM_GW = SSM_HPG * SSM_P
SSM_BC = 2 * SSM_GROUPS * SSM_N
SSM_CONV = 4
SSM_XBC = SSM_INNER + SSM_BC
D_FF = 5632
FFN_CONV = 3
ALPHA = (2.0 * DEPTH) ** 0.25
LN_EPS = 1e-5
RMS_EPS = 1e-6

LANES = 128
SUBLANES = 8
CHUNK = 128
N_LEVELS = int(math.log2(CHUNK))
TM = 512
TM_LN = 256
TM_PROJ_MAX = 1152
VMEM_LIMIT = 48 * 1024 * 1024
NEG_BIG = -1e30
LOG2_E = 1.4426950408889634

COL_Q, COL_F, COL_I, COL_G, COL_Z = 0, 2048, 4096, 6144, 8192
COL_XS, COL_BC = 10240, 12288
N_MAIN = 13312
COL_DT = 13312
COL_GATES = COL_DT + SSM_HEADS
COL_GA, COL_GB = N_MAIN, N_MAIN + D_MODEL


def _params(*semantics):
    return pltpu.CompilerParams(dimension_semantics=semantics, vmem_limit_bytes=VMEM_LIMIT)


def _nt(a, b):
    return lax.dot_general(a, b, (((1,), (1,)), ((), ())), preferred_element_type=F32)


def _dot(a, b):
    return jnp.dot(a, b, preferred_element_type=F32)


def _cumsum_rows(tri_b, g):
    n = g.shape[1]
    hi = g.astype(BF16)
    lo = (g - hi.astype(F32)).astype(BF16)
    s = _dot(tri_b, jnp.concatenate([hi, lo], axis=1))
    return s[:, :n] + s[:, n:]


def _silu(x):
    return x * jax.nn.sigmoid(x)


def _layer_norm(h, g, b):
    mu = jnp.mean(h, axis=-1, keepdims=True)
    d = h - mu
    var = jnp.mean(d * d, axis=-1, keepdims=True)
    return d * lax.rsqrt(var + LN_EPS) * g + b


def _pad_rows(x, rows):
    if x.shape[0] == rows:
        return x
    return jnp.concatenate([x, jnp.zeros((rows - x.shape[0],) + x.shape[1:], x.dtype)], axis=0)


def _lane_pair(row, h0):
    lane = lax.broadcasted_iota(jnp.int32, (1, LANES), 1)
    return jnp.where(lane < SSM_P, jnp.broadcast_to(row[:, h0:h0 + 1], (1, LANES)),
                     jnp.broadcast_to(row[:, h0 + 1:h0 + 2], (1, LANES)))


def _cast_kernel(x_ref, o_ref):
    o_ref[...] = x_ref[...].astype(o_ref.dtype)


def _cast_bf16(w, rows=512):
    lead, r, c = w.shape
    return pl.pallas_call(
        _cast_kernel,
        grid=(lead, r // rows),
        in_specs=[pl.BlockSpec((None, rows, c), lambda l, i: (l, i, 0))],
        out_specs=pl.BlockSpec((None, rows, c), lambda l, i: (l, i, 0)),
        out_shape=jax.ShapeDtypeStruct(w.shape, BF16),
        compiler_params=_params("arbitrary", "arbitrary"),
        name="cast_bf16",
    )(w)


def _slab_kernel(xs_ref, meta_ref, xp_ref, o_ref, ob_ref):
    i = pl.program_id(0)

    @pl.when(i == 0)
    def _():
        n_s, n_m = xs_ref.shape[0], meta_ref.shape[0]
        head = jnp.concatenate([xs_ref[...], meta_ref[...], jnp.zeros((TM - n_s - n_m, D_MODEL), F32)], axis=0)
        o_ref[...] = head
        ob_ref[...] = head.astype(ob_ref.dtype)

    @pl.when(i > 0)
    def _():
        o_ref[...] = xp_ref[...]
        ob_ref[...] = xp_ref[...].astype(ob_ref.dtype)


def _token_slab(x_sample, meta, x_prompt):
    m = TM + x_prompt.shape[0]
    fix = lambda i: (0, 0)
    return pl.pallas_call(
        _slab_kernel,
        grid=(m // TM,),
        in_specs=[pl.BlockSpec(x_sample.shape, fix), pl.BlockSpec(meta.shape, fix),
                  pl.BlockSpec((TM, D_MODEL), lambda i: (jnp.maximum(i - 1, 0), 0))],
        out_specs=[pl.BlockSpec((TM, D_MODEL), lambda i: (i, 0)), pl.BlockSpec((TM, D_MODEL), lambda i: (i, 0))],
        out_shape=[jax.ShapeDtypeStruct((m, D_MODEL), F32), jax.ShapeDtypeStruct((m, D_MODEL), BF16)],
        compiler_params=_params("arbitrary"),
        name="token_slab",
    )(x_sample, meta, x_prompt)


def _in_proj_kernel(x_ref, w_ref, o_ref, wb_ref):
    @pl.when(pl.program_id(1) == 0)
    def _():
        wb_ref[...] = w_ref[0].T.astype(BF16)

    o_ref[...] = _dot(x_ref[...], wb_ref[...])


def _row_tile(m, cap):
    return max(t for t in range(16, cap + 1, 16) if m % t == 0)


def _in_proj(xb, w_in_t, layer, tn=1024):
    m, k = xb.shape
    tm = _row_tile(m, TM_PROJ_MAX)
    n_main = N_MAIN // tn
    n_tiles = n_main + 2 * D_MODEL // tn
    assert COL_GATES % SUBLANES == 0 and tn % SUBLANES == 0

    def window(j, i):
        row8 = jnp.where(j < n_main, j * (tn // SUBLANES), COL_GATES // SUBLANES + (j - n_main) * (tn // SUBLANES))
        return (layer, SUBLANES * row8, 0)

    return pl.pallas_call(
        _in_proj_kernel,
        grid=(n_tiles, m // tm),
        in_specs=[pl.BlockSpec((tm, k), lambda j, i: (i, 0)),
                  pl.BlockSpec((pl.Element(1), pl.Element(tn), pl.Element(k)), window)],
        out_specs=pl.BlockSpec((tm, tn), lambda j, i: (i, j)),
        out_shape=jax.ShapeDtypeStruct((m, n_tiles * tn), F32),
        scratch_shapes=[pltpu.VMEM((k, tn), BF16)],
        compiler_params=_params("arbitrary", "arbitrary"),
        name="in_proj",
    )(xb, w_in_t)


def _level_boundary(cum, h):
    c = cum.shape[0]
    if h >= SUBLANES:
        blk = cum.reshape(c // (2 * h), 2 * h, LANES)
        return jnp.broadcast_to(blk[:, h - 1:h, :], blk.shape).reshape(c, LANES)
    tiles = cum.reshape(c // SUBLANES, SUBLANES, LANES)
    sub = lax.broadcasted_iota(jnp.int32, tiles.shape, 1)
    out = None
    for start in range(0, SUBLANES, 2 * h):
        row = jnp.broadcast_to(tiles[:, start + h - 1:start + h, :], tiles.shape)
        out = row if out is None else jnp.where(sub >= start, row, out)
    return out.reshape(c, LANES)


def _gla_kernel(q_ref, f_ref, i_ref, g_ref, lbl_ref, nw_ref, s0_ref, tri_ref, lvl_ref, buf_ref,
                o_ref, s_out_ref, st_ref, a_ref, *, layer, heads, rows, seqs):
    del buf_ref
    c = pl.program_id(2)
    last = pl.num_programs(2) - 1
    seg = CHUNK // seqs
    n_levels = int(math.log2(min(seg, rows)))

    @pl.when(c == 0)
    def _():
        for s in range(seqs):
            for h in range(heads):
                st_ref[s * heads + h] = s0_ref[s, h].T

    logits = lbl_ref[...]
    ex = jnp.exp(logits - jnp.max(logits, axis=0, keepdims=True))
    sm = ex / jnp.sum(ex, axis=0, keepdims=True)
    lb_all = jnp.zeros((1, sm.shape[1]), F32)
    for l in range(1, layer + 1):
        lb_all = lb_all + sm[l:l + 1]

    tri = tri_ref[...]
    lvl = lvl_ref[...]
    valid = lax.broadcasted_iota(jnp.int32, (CHUNK, LANES), 0) < rows

    head_cols = [slice(h * HG_D, (h + 1) * HG_D) for h in range(heads)]
    qs, ks, vs, decays, cums, o_inters = [], [], [], [], [], []
    for h, cols in enumerate(head_cols):
        lb = lb_all[:, cols]
        q = _silu(_pad_rows(q_ref[:, cols], CHUNK))
        f = lb + (1.0 - lb) * jax.nn.sigmoid(_pad_rows(f_ref[:, cols], CHUNK))
        decay = f if rows == CHUNK else jnp.where(valid, f, 1.0)
        qs.append(q)
        ks.append(1.0 - f)
        decays.append(decay)
        cums.append(_cumsum_rows(tri, jnp.log2(decay)))

    stream_of_lane = lax.broadcasted_iota(jnp.int32, (HG_D, CHUNK), 1) // seg
    for h, cols in enumerate(head_cols):
        q, k, cum = qs[h], ks[h], cums[h]
        v32 = _pad_rows(i_ref[:, cols], CHUNK)
        vs.append(v32.astype(BF16))
        q_dec = (q * jnp.exp2(cum)).astype(BF16)
        v_t = v32.T
        if seqs == 1:
            st = st_ref[h]
            o_inters.append(_nt(q_dec, st.astype(BF16)))
            c_last = cum[CHUNK - 1:CHUNK, :]
            k_dec = (k * jnp.exp2(c_last - cum)).astype(BF16)
            st_ref[h] = st * jnp.exp2(c_last) + _dot(v_t.astype(BF16), k_dec)
        else:
            ends = cum.reshape(seqs, seg, HG_D)[:, seg - 1:seg, :]
            c_last = jnp.broadcast_to(ends, (seqs, seg, HG_D)).reshape(CHUNK, HG_D)
            k_dec = (k * jnp.exp2(c_last - cum)).astype(BF16)
            parts = []
            for s in range(seqs):
                st = st_ref[s * heads + h]
                parts.append(_nt(q_dec[s * seg:(s + 1) * seg], st.astype(BF16)))
                v_s = jnp.where(stream_of_lane == s, v_t, 0.0).astype(BF16)
                st_ref[s * heads + h] = st * jnp.exp2(ends[s]) + _dot(v_s, k_dec)
            o_inters.append(jnp.concatenate(parts, axis=0))

    kbs = [k.astype(BF16) for k in ks]
    for h in range(heads):
        a_ref[h] = jnp.where(lvl == N_LEVELS, _nt(qs[h].astype(BF16), kbs[h]), 0.0)
    for h in range(heads):
        pltpu.store(a_ref.at[h], _nt((qs[h] * decays[h]).astype(BF16), kbs[h]), mask=lvl == 0)
    for lev in range(1, n_levels):
        half = 1 << lev
        for h in range(heads):
            q, k, cum = qs[h], ks[h], cums[h]
            if half < SUBLANES:
                z = jnp.exp2(-jnp.abs(cum - _level_boundary(cum, half)))
                pltpu.store(a_ref.at[h], _nt((q * z).astype(BF16), (k * z).astype(BF16)), mask=lvl == lev)
            else:
                q_parts, k_parts, uppers = [], [], []
                for start in range(0, CHUNK, 2 * half):
                    lo, up = slice(start, start + half), slice(start + half, start + 2 * half)
                    mid = cum[start + half - 1:start + half, :]
                    q_parts.append(q[up] * jnp.exp2(cum[up] - mid))
                    k_parts += [k[lo] * jnp.exp2(mid - cum[lo]), k[up]]
                    uppers.append(up)
                p = _nt(jnp.concatenate(q_parts, axis=0).astype(BF16),
                        jnp.concatenate(k_parts, axis=0).astype(BF16))
                for n, up in enumerate(uppers):
                    pltpu.store(a_ref.at[h, up, :], p[n * half:(n + 1) * half, :], mask=lvl[up, :] == lev)

    for h, cols in enumerate(head_cols):
        o = (_dot(a_ref[h].astype(BF16), vs[h]) + o_inters[h])[:rows]
        o = o * lax.rsqrt(jnp.mean(o * o, axis=-1, keepdims=True) + RMS_EPS) * nw_ref[layer:layer + 1, cols]
        o_ref[:, cols] = (o * _silu(g_ref[:, cols])).astype(o_ref.dtype)

    @pl.when(c == last)
    def _():
        for s in range(seqs):
            for h in range(heads):
                s_out_ref[s, h] = st_ref[s * heads + h].T


def _gla(proj, buf, lb_logits, norm_w, s0, consts, *, layer, batch, seq, row0, heads_per_step=16):
    seqs = CHUNK // seq if seq < CHUNK and CHUNK % seq == 0 and batch % (CHUNK // seq) == 0 else 1
    rows = CHUNK if seqs > 1 else min(seq, CHUNK)
    nc = max(seq // CHUNK, 1)
    blk0 = row0 // rows
    hp = heads_per_step
    w = hp * HG_D
    tri = consts["tri"] if seqs == 1 else consts["tri_streams"][seqs]

    def col(base):
        return lambda b, hb, c: (blk0 + b * nc + c, base // w + hb)

    par = lambda b, hb, c: (0, hb)
    state = lambda b, hb, c: (b, hb, 0, 0)
    out_rows = lambda b, hb, c: (blk0 + b * nc + c, hb)
    return pl.pallas_call(
        functools.partial(_gla_kernel, layer=layer, heads=hp, rows=rows, seqs=seqs),
        grid=(batch // seqs, HG_HEADS // hp, nc),
        in_specs=[pl.BlockSpec((rows, w), col(COL_Q)),
                  pl.BlockSpec((rows, w), col(COL_F)),
                  pl.BlockSpec((rows, w), col(COL_I)),
                  pl.BlockSpec((rows, w), col(COL_G)),
                  pl.BlockSpec((DEPTH, w), par),
                  pl.BlockSpec((DEPTH, w), par),
                  pl.BlockSpec((seqs, hp, HG_D, HG_D), state),
                  pl.BlockSpec((CHUNK, CHUNK), lambda b, hb, c: (0, 0)),
                  pl.BlockSpec((CHUNK, CHUNK), lambda b, hb, c: (0, 0)),
                  pl.BlockSpec(memory_space=pl.ANY)],
        out_specs=[pl.BlockSpec((rows, w), out_rows),
                   pl.BlockSpec((seqs, hp, HG_D, HG_D), state)],
        out_shape=[jax.ShapeDtypeStruct(buf.shape, buf.dtype),
                   jax.ShapeDtypeStruct((batch, HG_HEADS, HG_D, HG_D), F32)],
        scratch_shapes=[pltpu.VMEM((seqs * hp, HG_D, HG_D), F32), pltpu.VMEM((hp, CHUNK, CHUNK), F32)],
        input_output_aliases={9: 0},
        compiler_params=_params("arbitrary", "arbitrary", "arbitrary"),
        name="hgrn2",
    )(proj, proj, proj, proj, lb_logits, norm_w, s0, tri, consts["level"], buf)


def _causal_conv4(raw, ext_ref, w, bias):
    n = raw.shape[0]
    ext_ref[SUBLANES:SUBLANES + n, :] = raw
    out = (bias + w[3:4] * raw + w[2:3] * ext_ref[7:7 + n, :] + w[1:2] * ext_ref[6:6 + n, :]
           + w[0:1] * ext_ref[5:5 + n, :])
    ext_ref[0:SUBLANES, :] = raw[n - SUBLANES:]
    return out


def _ssd_kernel(xs_ref, bc_ref, z_ref, x_ref, wdt_ref, dtb_ref, cw_ref, cb_ref, st_ref, alog_ref, dskip_ref,
                nw_ref, h0_ref, tri_ref, buf_ref, y_ref, h_out_ref, st_out_ref, ht_ref, px_ref, pbc_ref,
                wdtb_ref, *, layer, rows):
    del buf_ref
    c = pl.program_id(1)
    last = pl.num_programs(1) - 1
    tail = SSM_CONV - 1

    @pl.when(c == 0)
    def _():
        wdtb_ref[...] = wdt_ref[...].T.astype(BF16)
        ht_ref[...] = h0_ref[0].T
        buf = jnp.concatenate([jnp.zeros((SUBLANES - tail, SSM_XBC), F32), st_ref[0]], axis=0)
        px_ref[0:SUBLANES, :] = buf[:, :SSM_INNER]
        pbc_ref[0:SUBLANES, :] = buf[:, SSM_INNER:]

    raw_x, raw_bc = xs_ref[...], bc_ref[...]
    st_out_ref[0, :, :SSM_INNER] = raw_x[rows - tail:]
    st_out_ref[0, :, SSM_INNER:] = raw_bc[rows - tail:]
    cw = cw_ref[...]
    cb = cb_ref[layer:layer + 1, :]
    xs = _pad_rows(_silu(_causal_conv4(raw_x, px_ref, cw[:, :SSM_INNER], cb[:, :SSM_INNER])), CHUNK)
    bc = _pad_rows(_silu(_causal_conv4(raw_bc, pbc_ref, cw[:, SSM_INNER:], cb[:, SSM_INNER:])), CHUNK)

    dt_raw = _dot(x_ref[...], wdtb_ref[...]) + dtb_ref[layer:layer + 1, :]
    dt = _pad_rows(jnp.maximum(dt_raw, 0.0) + jnp.log1p(jnp.exp(-jnp.abs(dt_raw))), CHUNK)
    if rows < CHUNK:
        dt = jnp.where(lax.broadcasted_iota(jnp.int32, dt.shape, 0) < rows, dt, 0.0)
    a = dt * (-LOG2_E * jnp.exp(alog_ref[layer:layer + 1, :]))
    cum = _cumsum_rows(tri_ref[...], a)
    c_last = cum[CHUNK - 1:CHUNK, :]
    cum_t = cum.T
    dt_t = dt.T
    wgt_t = (jnp.exp2(c_last - cum) * dt).T
    dec = jnp.exp2(c_last)
    dskip = dskip_ref[layer:layer + 1, :]

    causal = (lax.broadcasted_iota(jnp.int32, (CHUNK, CHUNK), 0)
              >= lax.broadcasted_iota(jnp.int32, (CHUNK, CHUNK), 1))
    low = lax.broadcasted_iota(jnp.int32, (CHUNK, LANES), 1) < SSM_P
    z = z_ref[...]
    for g in range(SSM_GROUPS):
        bm = bc[:, g * SSM_N:(g + 1) * SSM_N]
        cm = bc[:, (SSM_GROUPS + g) * SSM_N:(SSM_GROUPS + g + 1) * SSM_N]
        cb_ts = _nt(cm.astype(BF16), bm.astype(BF16))
        bm_t = bm.T
        y_parts = []
        for pair in range(SSM_HPG // 2):
            h0 = g * SSM_HPG + 2 * pair
            cols = slice((h0 // 2) * LANES, (h0 // 2 + 1) * LANES)
            xp = xs[:, cols]
            ht = ht_ref[:, cols]
            lhs, rhs, upd_l, upd_r = [], [], [], []
            for h, mine in ((h0, low), (h0 + 1, jnp.logical_not(low))):
                col_b = jnp.broadcast_to(cum[:, h:h + 1], (CHUNK, CHUNK))
                lmat = jnp.exp2(jnp.where(causal, col_b - cum_t[h:h + 1, :], NEG_BIG))
                lhs.append((cb_ts * lmat * dt_t[h:h + 1, :]).astype(BF16))
                lhs.append((cm * jnp.exp2(col_b)).astype(BF16))
                xh = jnp.where(mine, xp, 0.0).astype(BF16)
                rhs.append(xh)
                rhs.append(jnp.where(mine, ht, 0.0).astype(BF16))
                upd_l.append((bm_t * wgt_t[h:h + 1, :]).astype(BF16))
                upd_r.append(xh)
            y_parts.append(_dot(jnp.concatenate(lhs, axis=1), jnp.concatenate(rhs, axis=0)))
            ht_ref[:, cols] = ht * _lane_pair(dec, h0) + _dot(jnp.concatenate(upd_l, axis=1),
                                                             jnp.concatenate(upd_r, axis=0))
            y_parts[-1] = y_parts[-1] + _lane_pair(dskip, h0) * xp
        gcols = slice(g * SSM_GW, (g + 1) * SSM_GW)
        y = jnp.concatenate(y_parts, axis=1)[:rows] * _silu(z[:, gcols])
        y = y * lax.rsqrt(jnp.mean(y * y, axis=-1, keepdims=True) + RMS_EPS) * nw_ref[layer:layer + 1, gcols]
        y_ref[:, gcols] = y.astype(y_ref.dtype)

    @pl.when(c == last)
    def _():
        h_out_ref[0] = ht_ref[...].T


def _ssd(proj, xb, buf, p, h0, conv0, consts, *, layer, batch, seq, row0):
    rows = min(seq, CHUNK)
    nc = seq // rows
    blk0 = row0 // rows
    n = SSM_N
    tail = SSM_CONV - 1
    h0 = h0.reshape(batch, SSM_INNER, n)
    row = lambda base, width: (lambda b, c: (blk0 + b * nc + c, base // width))
    par = lambda b, c: (0, 0)
    st = lambda b, c: (b, 0, 0)
    y, h_out, conv_out = pl.pallas_call(
        functools.partial(_ssd_kernel, layer=layer, rows=rows),
        grid=(batch, nc),
        in_specs=[pl.BlockSpec((rows, SSM_INNER), row(COL_XS, SSM_INNER)),
                  pl.BlockSpec((rows, SSM_BC), row(COL_BC, SSM_BC)),
                  pl.BlockSpec((rows, SSM_INNER), row(COL_Z, SSM_INNER)),
                  pl.BlockSpec((rows, D_MODEL), row(0, D_MODEL)),
                  pl.BlockSpec((None, LANES, D_MODEL), lambda b, c: (layer, COL_DT // LANES, 0)),
                  pl.BlockSpec((DEPTH, LANES), par),
                  pl.BlockSpec((None, SSM_CONV, SSM_XBC), lambda b, c: (layer, 0, 0)),
                  pl.BlockSpec((DEPTH, SSM_XBC), par),
                  pl.BlockSpec((1, tail, SSM_XBC), st),
                  pl.BlockSpec((DEPTH, LANES), par),
                  pl.BlockSpec((DEPTH, LANES), par),
                  pl.BlockSpec((DEPTH, SSM_INNER), par),
                  pl.BlockSpec((1, SSM_INNER, n), st),
                  pl.BlockSpec((CHUNK, CHUNK), par),
                  pl.BlockSpec(memory_space=pl.ANY)],
        out_specs=[pl.BlockSpec((rows, SSM_INNER), row(0, SSM_INNER)),
                   pl.BlockSpec((1, SSM_INNER, n), st),
                   pl.BlockSpec((1, tail, SSM_XBC), st)],
        out_shape=[jax.ShapeDtypeStruct(buf.shape, buf.dtype),
                   jax.ShapeDtypeStruct((batch, SSM_INNER, n), F32),
                   jax.ShapeDtypeStruct((batch, tail, SSM_XBC), F32)],
        scratch_shapes=[pltpu.VMEM((n, SSM_INNER), F32),
                        pltpu.VMEM((SUBLANES + rows, SSM_INNER), F32),
                        pltpu.VMEM((SUBLANES + rows, SSM_BC), F32),
                        pltpu.VMEM((D_MODEL, LANES), BF16)],
        input_output_aliases={14: 0},
        compiler_params=_params("arbitrary", "arbitrary"),
        name="ssd",
    )(proj, proj, proj, xb, p["w_in_t"], p["dt_bias"], p["ssm_conv_w"], p["ssm_conv_b"], conv0, p["a_log"],
      p["d_skip"], p["ssm_norm_w"], h0, consts["tri"], buf)
    return y, h_out.reshape(batch, SSM_HEADS, SSM_P, n), conv_out


def _merge_kernel(oa_ref, ob_ref, wa_ref, wb_ref, ga_ref, gb_ref, o_ref, wab_ref, wbb_ref):
    @pl.when(pl.program_id(1) == 0)
    def _():
        wab_ref[...] = wa_ref[...].astype(BF16)
        wbb_ref[...] = wb_ref[...].astype(BF16)

    o_ref[...] = (jax.nn.sigmoid(ga_ref[...]) * _dot(oa_ref[...], wab_ref[...])
                  + jax.nn.sigmoid(gb_ref[...]) * _dot(ob_ref[...], wbb_ref[...])).astype(o_ref.dtype)


def _merge(o_a, o_b, w_a, w_b, proj, layer, tn=1024):
    m, k = o_a.shape
    wspec = pl.BlockSpec((None, k, tn), lambda j, i: (layer, 0, j), pipeline_mode=pl.Buffered(1))
    return pl.pallas_call(
        _merge_kernel,
        grid=(D_MODEL // tn, m // TM),
        in_specs=[pl.BlockSpec((TM, k), lambda j, i: (i, 0)),
                  pl.BlockSpec((TM, k), lambda j, i: (i, 0)),
                  wspec, wspec,
                  pl.BlockSpec((TM, tn), lambda j, i: (i, COL_GA // tn + j)),
                  pl.BlockSpec((TM, tn), lambda j, i: (i, COL_GB // tn + j))],
        out_specs=pl.BlockSpec((TM, tn), lambda j, i: (i, j)),
        out_shape=jax.ShapeDtypeStruct((m, D_MODEL), BF16),
        scratch_shapes=[pltpu.VMEM((k, tn), BF16), pltpu.VMEM((k, tn), BF16)],
        compiler_params=_params("arbitrary", "arbitrary"),
        name="merge",
    )(o_a, o_b, w_a, w_b, proj, proj)


def _out_ln_kernel(mg_ref, w_ref, x_ref, g_ref, b_ref, o_ref, ob_ref, wb_ref, *, layer):
    @pl.when(pl.program_id(0) == 0)
    def _():
        wb_ref[...] = w_ref[...].astype(BF16)

    h = ALPHA * x_ref[...] + _dot(mg_ref[...], wb_ref[...])
    y = _layer_norm(h, g_ref[layer:layer + 1, :], b_ref[layer:layer + 1, :])
    o_ref[...] = y
    ob_ref[...] = y.astype(ob_ref.dtype)


def _out_ln(merged, w_out, x, g, b, layer):
    m, k = merged.shape
    row = lambda i: (i, 0)
    fix = lambda i: (0, 0)
    return pl.pallas_call(
        functools.partial(_out_ln_kernel, layer=layer),
        grid=(m // TM_LN,),
        in_specs=[pl.BlockSpec((TM_LN, k), row),
                  pl.BlockSpec((None, k, D_MODEL), lambda i: (layer, 0, 0), pipeline_mode=pl.Buffered(1)),
                  pl.BlockSpec((TM_LN, D_MODEL), row),
                  pl.BlockSpec((DEPTH, D_MODEL), fix),
                  pl.BlockSpec((DEPTH, D_MODEL), fix)],
        out_specs=[pl.BlockSpec((TM_LN, D_MODEL), row), pl.BlockSpec((TM_LN, D_MODEL), row)],
        out_shape=[jax.ShapeDtypeStruct((m, D_MODEL), F32), jax.ShapeDtypeStruct((m, D_MODEL), BF16)],
        scratch_shapes=[pltpu.VMEM((k, D_MODEL), BF16)],
        compiler_params=_params("arbitrary"),
        name="out_ln",
    )(merged, w_out, x, g, b)


def _gelu(x):
    return 0.5 * x * (1.0 + lax.erf(x * (1.0 / math.sqrt(2.0))))


def _ffn_up_kernel(x_ref, wa_ref, wv_ref, cw_ref, cb_ref, st_ref, h_ref, st_s_ref, st_p_ref,
                   wab_ref, wvb_ref, prev_ref, meta_ref, *, layer, slot_rows, meta_slot, tiles_per_seq):
    i = pl.program_id(1)
    tn = wab_ref.shape[1]
    tail = FFN_CONV - 1

    @pl.when(i == 0)
    def _():
        wab_ref[...] = wa_ref[...].astype(BF16)
        wvb_ref[...] = wv_ref[...].astype(BF16)

    x = x_ref[...]
    a = _dot(x, wab_ref[...])
    w = cw_ref[...]
    bias = cb_ref[layer:layer + 1, :]

    def gated(conv):
        act = _gelu(conv).reshape(TM, tn)
        return (act * _dot(x, wvb_ref[...])).astype(h_ref.dtype)

    @pl.when(i == 0)
    def _():
        slots = TM // slot_rows
        n_s = st_ref.shape[0]
        a3 = a.reshape(slots, slot_rows, tn)
        prev = jnp.concatenate(
            [jnp.concatenate([jnp.zeros((n_s, SUBLANES - tail, tn), F32), st_ref[...]], axis=1),
             jnp.zeros((slots - n_s, SUBLANES, tn), F32)], axis=0)
        ext = jnp.concatenate([prev, a3], axis=1)
        n = slot_rows
        conv = bias + w[2:3] * a3 + w[1:2] * ext[:, 7:7 + n] + w[0:1] * ext[:, 6:6 + n]
        st_s_ref[...] = a3[:n_s, n - tail:]
        meta_ref[...] = a3[meta_slot, N_META - tail:N_META]
        h_ref[...] = gated(conv)

    @pl.when(i > 0)
    def _():
        @pl.when((i - 1) % tiles_per_seq == 0)
        def _():
            prev_ref[...] = jnp.concatenate([jnp.zeros((SUBLANES - tail, tn), F32), meta_ref[...]], axis=0)

        ext = jnp.concatenate([prev_ref[...], a], axis=0)
        conv = bias + w[2:3] * a + w[1:2] * ext[7:7 + TM] + w[0:1] * ext[6:6 + TM]
        prev_ref[...] = a[TM - SUBLANES:]
        st_p_ref[0] = a[TM - tail:]
        h_ref[...] = gated(conv)


def _ffn_up(xb, w_up, conv_w, conv_b, st_sample, layer, *, batch, seq, slot_rows, meta_slot, tn=512):
    m, k = xb.shape
    nj = D_FF // tn
    tail = FFN_CONV - 1
    tps = seq // TM
    n_s = st_sample.shape[1]
    return pl.pallas_call(
        functools.partial(_ffn_up_kernel, layer=layer, slot_rows=slot_rows, meta_slot=meta_slot,
                          tiles_per_seq=tps),
        grid=(nj, m // TM),
        in_specs=[pl.BlockSpec((TM, k), lambda j, i: (i, 0)),
                  pl.BlockSpec((None, k, tn), lambda j, i: (layer, 0, j)),
                  pl.BlockSpec((None, k, tn), lambda j, i: (layer, 0, nj + j)),
                  pl.BlockSpec((None, FFN_CONV, tn), lambda j, i: (layer, 0, j)),
                  pl.BlockSpec((DEPTH, tn), lambda j, i: (0, j)),
                  pl.BlockSpec((None, n_s, tail, tn), lambda j, i: (layer, 0, 0, j))],
        out_specs=[pl.BlockSpec((TM, tn), lambda j, i: (i, j)),
                   pl.BlockSpec((n_s, tail, tn), lambda j, i: (0, 0, j)),
                   pl.BlockSpec((1, tail, tn), lambda j, i: (jnp.maximum(i - 1, 0) // tps, 0, j))],
        out_shape=[jax.ShapeDtypeStruct((m, D_FF), BF16),
                   jax.ShapeDtypeStruct((n_s, tail, D_FF), F32),
                   jax.ShapeDtypeStruct((batch, tail, D_FF), F32)],
        scratch_shapes=[pltpu.VMEM((k, tn), BF16), pltpu.VMEM((k, tn), BF16),
                        pltpu.VMEM((SUBLANES, tn), F32), pltpu.VMEM((tail, tn), F32)],
        compiler_params=_params("arbitrary", "arbitrary"),
        name="ffn_up",
    )(xb, w_up, w_up, conv_w, conv_b, st_sample)


def _ffn_down_kernel(h_ref, w_ref, x_ref, g_ref, b_ref, o_ref, ob_ref, *, layer):
    y = _layer_norm(ALPHA * x_ref[...] + _dot(h_ref[...], w_ref[...]),
                    g_ref[layer:layer + 1, :], b_ref[layer:layer + 1, :])
    o_ref[...] = y
    ob_ref[...] = y.astype(ob_ref.dtype)


def _ffn_down_final_kernel(h_ref, w_ref, x_ref, g_ref, b_ref, ys_ref, yp_ref, *, layer, prompt_tile0):
    i = pl.program_id(0)
    y = _layer_norm(ALPHA * x_ref[...] + _dot(h_ref[...], w_ref[...]),
                    g_ref[layer:layer + 1, :], b_ref[layer:layer + 1, :])

    @pl.when(i == 0)
    def _():
        ys_ref[...] = y

    @pl.when(i >= prompt_tile0)
    def _():
        yp_ref[...] = y


def _ffn_down(h, w_down_b, x, g, b, layer, *, final, n_sample, row_prompt):
    m, k = h.shape
    row = lambda i: (i, 0)
    fix = lambda i: (0, 0)
    in_specs = [pl.BlockSpec((TM_LN, k), row),
                pl.BlockSpec((None, k, D_MODEL), lambda i: (layer, 0, 0), pipeline_mode=pl.Buffered(1)),
                pl.BlockSpec((TM_LN, D_MODEL), row),
                pl.BlockSpec((DEPTH, D_MODEL), fix),
                pl.BlockSpec((DEPTH, D_MODEL), fix)]
    if not final:
        return pl.pallas_call(
            functools.partial(_ffn_down_kernel, layer=layer),
            grid=(m // TM_LN,),
            in_specs=in_specs,
            out_specs=[pl.BlockSpec((TM_LN, D_MODEL), row), pl.BlockSpec((TM_LN, D_MODEL), row)],
            out_shape=[jax.ShapeDtypeStruct((m, D_MODEL), F32), jax.ShapeDtypeStruct((m, D_MODEL), BF16)],
            compiler_params=_params("arbitrary"),
            name="ffn_down",
        )(h, w_down_b, x, g, b)
    assert n_sample == TM_LN and row_prompt % TM_LN == 0
    t0 = row_prompt // TM_LN
    return pl.pallas_call(
        functools.partial(_ffn_down_final_kernel, layer=layer, prompt_tile0=t0),
        grid=(m // TM_LN,),
        in_specs=in_specs,
        out_specs=[pl.BlockSpec((TM_LN, D_MODEL), fix),
                   pl.BlockSpec((TM_LN, D_MODEL), lambda i: (jnp.maximum(i - t0, 0), 0))],
        out_shape=[jax.ShapeDtypeStruct((n_sample, D_MODEL), F32),
                   jax.ShapeDtypeStruct((m - row_prompt, D_MODEL), F32)],
        compiler_params=_params("arbitrary"),
        name="ffn_down_final",
    )(h, w_down_b, x, g, b)


def _constants():
    t = jnp.arange(CHUNK)[:, None]
    s = jnp.arange(CHUNK)[None, :]
    level = jnp.full((CHUNK, CHUNK), -1, jnp.int32)
    for lev in range(N_LEVELS):
        level = jnp.where((t > s) & (((t ^ s) >> lev) == 1), lev, level)
    level = jnp.where(t == s, N_LEVELS, level)
    tri_streams = {n: ((t >= s) & (t // (CHUNK // n) == s // (CHUNK // n))).astype(BF16) for n in (2, 4, 8)}
    return dict(tri=(t >= s).astype(BF16), level=level, tri_streams=tri_streams)


def _head_lanes(v):
    return jnp.pad(v.astype(F32), ((0, 0), (0, LANES - SSM_HEADS)))


def kernel(x_prompt, x_sample, state_hgrn, state_ssm, state_ssm_conv, state_ffn_conv, meta_tokens, w_in,
           hgrn_lb_logits, hgrn_norm_w, w_proj_a, ssm_conv_w, ssm_conv_b, ssm_dt_bias, ssm_a_log, ssm_d,
           ssm_norm_w, w_proj_b, w_out, ln1_g, ln1_b, ffn_w_up, ffn_conv_w, ffn_conv_b, ffn_w_down, ln2_g, ln2_b):
    batch, seq, _ = x_prompt.shape
    dec_batch, dec_seq, _ = x_sample.shape
    n_sample = dec_batch * dec_seq
    row_meta = n_sample
    row_prompt = TM
    assert seq % TM == 0 and dec_seq >= N_META and row_meta + dec_seq <= TM and row_meta % dec_seq == 0
    m = row_prompt + batch * seq

    x, xb = _token_slab(x_sample.reshape(n_sample, D_MODEL).astype(F32), meta_tokens.astype(F32),
                        x_prompt.reshape(batch * seq, D_MODEL).astype(F32))
    consts = _constants()
    lb_logits = hgrn_lb_logits.astype(F32)
    w_in_t = jnp.swapaxes(w_in, 1, 2)
    ssm_p = dict(ssm_conv_w=ssm_conv_w, ssm_conv_b=ssm_conv_b, a_log=_head_lanes(ssm_a_log),
                 d_skip=_head_lanes(ssm_d), ssm_norm_w=ssm_norm_w, w_in_t=w_in_t,
                 dt_bias=_head_lanes(ssm_dt_bias))
    w_down_b = _cast_bf16(ffn_w_down)
    groups = (dict(batch=1, seq=N_META, row0=row_meta), dict(batch=batch, seq=seq, row0=row_prompt),
              dict(batch=dec_batch, seq=dec_seq, row0=0))

    o_a = jnp.zeros((m, HG_HEADS * HG_D), BF16)
    o_b = jnp.zeros((m, SSM_INNER), BF16)

    hg_p, ssm_pp, conv_p, ffn_p, hg_s, ssm_s, conv_s, ffn_s = ([] for _ in range(8))
    y_sample = y_prompt = None
    for l in range(DEPTH):
        proj = _in_proj(xb, w_in_t, l)

        o_a, hg_m = _gla(proj, o_a, lb_logits, hgrn_norm_w, jnp.zeros((1, HG_HEADS, HG_D, HG_D), F32), consts,
                         layer=l, **groups[0])
        o_b, ssm_m, conv_m = _ssd(proj, xb, o_b, ssm_p, jnp.zeros((1, SSM_HEADS, SSM_P, SSM_N), F32),
                                  jnp.zeros((1, SSM_CONV - 1, SSM_XBC), F32), consts, layer=l, **groups[0])
        seed = lambda s: jnp.broadcast_to(s, (batch,) + s.shape[1:])
        o_a, s_hg = _gla(proj, o_a, lb_logits, hgrn_norm_w, seed(hg_m), consts, layer=l, **groups[1])
        o_b, s_ssm, s_conv = _ssd(proj, xb, o_b, ssm_p, seed(ssm_m), seed(conv_m), consts, layer=l, **groups[1])
        hg_p.append(s_hg)
        ssm_pp.append(s_ssm)
        conv_p.append(s_conv)
        o_a, s_hg = _gla(proj, o_a, lb_logits, hgrn_norm_w, state_hgrn[l], consts, layer=l, **groups[2])
        o_b, s_ssm, s_conv = _ssd(proj, xb, o_b, ssm_p, state_ssm[l], state_ssm_conv[l], consts, layer=l,
                                  **groups[2])
        hg_s.append(s_hg)
        ssm_s.append(s_ssm)
        conv_s.append(s_conv)

        merged = _merge(o_a, o_b, w_proj_a, w_proj_b, proj, l)
        x, xb = _out_ln(merged, w_out, x, ln1_g, ln1_b, l)
        hid, f_s, f_p = _ffn_up(xb, ffn_w_up, ffn_conv_w, ffn_conv_b, state_ffn_conv, l, batch=batch, seq=seq,
                                slot_rows=dec_seq, meta_slot=row_meta // dec_seq)
        ffn_s.append(f_s)
        ffn_p.append(f_p)
        if l < DEPTH - 1:
            x, xb = _ffn_down(hid, w_down_b, x, ln2_g, ln2_b, l, final=False, n_sample=n_sample,
                              row_prompt=row_prompt)
        else:
            y_sample, y_prompt = _ffn_down(hid, w_down_b, x, ln2_g, ln2_b, l, final=True, n_sample=n_sample,
                                           row_prompt=row_prompt)
    return (y_prompt.reshape(batch, seq, D_MODEL), y_sample.reshape(dec_batch, dec_seq, D_MODEL),
            jnp.stack(hg_p), jnp.stack(ssm_pp), jnp.stack(conv_p), jnp.stack(ffn_p),
            jnp.stack(hg_s), jnp.stack(ssm_s), jnp.stack(conv_s), jnp.stack(ffn_s))
```

```python
import functools
import math

import jax
import jax.numpy as jnp
from jax import lax
from jax.experimental import pallas as pl
from jax.experimental.pallas import tpu as pltpu

F32 = jnp.float32
BF16 = jnp.bfloat16

D_MODEL = 2048
DEPTH = 2
N_META = 16
HG_HEADS = 16
HG_D = 128
SSM_HEADS = 32
SSM_P = 64
SSM_GROUPS = 4
SSM_HPG = SSM_HEADS // SSM_GROUPS
SSM_N = 128
SSM_INNER = SSM_HEADS * SSM_P
SSM_GW = SSM_HPG * SSM_P
SSM_BC = 2 * SSM_GROUPS * SSM_N
SSM_CONV = 4
SSM_XBC = SSM_INNER + SSM_BC
D_FF = 5632
FFN_CONV = 3
ALPHA = (2.0 * DEPTH) ** 0.25
LN_EPS = 1e-5
RMS_EPS = 1e-6

LANES = 128
SUBLANES = 8
CHUNK = 128
N_LEVELS = int(math.log2(CHUNK))
TM = 512
TM_LN = 256
TM_PROJ_MAX = 1152
VMEM_LIMIT = 48 * 1024 * 1024
NEG_BIG = -1e30
LOG2_E = 1.4426950408889634

COL_Q, COL_F, COL_I, COL_G, COL_Z = 0, 2048, 4096, 6144, 8192
COL_XS, COL_BC = 10240, 12288
N_MAIN = 13312
COL_DT = 13312
COL_GATES = COL_DT + SSM_HEADS
COL_GA, COL_GB = N_MAIN, N_MAIN + D_MODEL


def _params(*semantics):
    return pltpu.CompilerParams(dimension_semantics=semantics, vmem_limit_bytes=VMEM_LIMIT)


def _nt(a, b):
    return lax.dot_general(a, b, (((1,), (1,)), ((), ())), preferred_element_type=F32)


def _dot(a, b):
    return jnp.dot(a, b, preferred_element_type=F32)


def _cumsum_rows(tri_b, g):
    n = g.shape[1]
    hi = g.astype(BF16)
    lo = (g - hi.astype(F32)).astype(BF16)
    s = _dot(tri_b, jnp.concatenate([hi, lo], axis=1))
    return s[:, :n] + s[:, n:]


def _silu(x):
    return x * jax.nn.sigmoid(x)


def _layer_norm(h, g, b):
    mu = jnp.mean(h, axis=-1, keepdims=True)
    d = h - mu
    var = jnp.mean(d * d, axis=-1, keepdims=True)
    return d * lax.rsqrt(var + LN_EPS) * g + b


def _pad_rows(x, rows):
    if x.shape[0] == rows:
        return x
    return jnp.concatenate([x, jnp.zeros((rows - x.shape[0],) + x.shape[1:], x.dtype)], axis=0)


def _lane_pair(row, h0):
    lane = lax.broadcasted_iota(jnp.int32, (1, LANES), 1)
    return jnp.where(lane < SSM_P, jnp.broadcast_to(row[:, h0:h0 + 1], (1, LANES)),
                     jnp.broadcast_to(row[:, h0 + 1:h0 + 2], (1, LANES)))


def _cast_kernel(x_ref, o_ref):
    o_ref[...] = x_ref[...].astype(o_ref.dtype)


def _cast_bf16(w, rows=512):
    lead, r, c = w.shape
    return pl.pallas_call(
        _cast_kernel,
        grid=(lead, r // rows),
        in_specs=[pl.BlockSpec((None, rows, c), lambda l, i: (l, i, 0))],
        out_specs=pl.BlockSpec((None, rows, c), lambda l, i: (l, i, 0)),
        out_shape=jax.ShapeDtypeStruct(w.shape, BF16),
        compiler_params=_params("arbitrary", "arbitrary"),
        name="cast_bf16",
    )(w)


def _slab_kernel(xs_ref, meta_ref, xp_ref, o_ref, ob_ref):
    i = pl.program_id(0)

    @pl.when(i == 0)
    def _():
        n_s, n_m = xs_ref.shape[0], meta_ref.shape[0]
        head = jnp.concatenate([xs_ref[...], meta_ref[...], jnp.zeros((TM - n_s - n_m, D_MODEL), F32)], axis=0)
        o_ref[...] = head
        ob_ref[...] = head.astype(ob_ref.dtype)

    @pl.when(i > 0)
    def _():
        o_ref[...] = xp_ref[...]
        ob_ref[...] = xp_ref[...].astype(ob_ref.dtype)


def _token_slab(x_sample, meta, x_prompt):
    m = TM + x_prompt.shape[0]
    fix = lambda i: (0, 0)
    return pl.pallas_call(
        _slab_kernel,
        grid=(m // TM,),
        in_specs=[pl.BlockSpec(x_sample.shape, fix), pl.BlockSpec(meta.shape, fix),
                  pl.BlockSpec((TM, D_MODEL), lambda i: (jnp.maximum(i - 1, 0), 0))],
        out_specs=[pl.BlockSpec((TM, D_MODEL), lambda i: (i, 0)), pl.BlockSpec((TM, D_MODEL), lambda i: (i, 0))],
        out_shape=[jax.ShapeDtypeStruct((m, D_MODEL), F32), jax.ShapeDtypeStruct((m, D_MODEL), BF16)],
        compiler_params=_params("arbitrary"),
        name="token_slab",
    )(x_sample, meta, x_prompt)


def _in_proj_kernel(x_ref, w_ref, o_ref, wb_ref):
    @pl.when(pl.program_id(1) == 0)
    def _():
        wb_ref[...] = w_ref[0].T.astype(BF16)

    o_ref[...] = _dot(x_ref[...], wb_ref[...])


def _row_tile(m, cap):
    return max(t for t in range(16, cap + 1, 16) if m % t == 0)


def _in_proj(xb, w_in_t, layer, tn=1024):
    m, k = xb.shape
    tm = _row_tile(m, TM_PROJ_MAX)
    n_main = N_MAIN // tn
    n_tiles = n_main + 2 * D_MODEL // tn
    assert COL_GATES % SUBLANES == 0 and tn % SUBLANES == 0

    def window(j, i):
        row8 = jnp.where(j < n_main, j * (tn // SUBLANES), COL_GATES // SUBLANES + (j - n_main) * (tn // SUBLANES))
        return (layer, SUBLANES * row8, 0)

    return pl.pallas_call(
        _in_proj_kernel,
        grid=(n_tiles, m // tm),
        in_specs=[pl.BlockSpec((tm, k), lambda j, i: (i, 0)),
                  pl.BlockSpec((pl.Element(1), pl.Element(tn), pl.Element(k)), window)],
        out_specs=pl.BlockSpec((tm, tn), lambda j, i: (i, j)),
        out_shape=jax.ShapeDtypeStruct((m, n_tiles * tn), F32),
        scratch_shapes=[pltpu.VMEM((k, tn), BF16)],
        compiler_params=_params("arbitrary", "arbitrary"),
        name="in_proj",
    )(xb, w_in_t)


def _level_boundary(cum, h):
    c = cum.shape[0]
    if h >= SUBLANES:
        blk = cum.reshape(c // (2 * h), 2 * h, LANES)
        return jnp.broadcast_to(blk[:, h - 1:h, :], blk.shape).reshape(c, LANES)
    tiles = cum.reshape(c // SUBLANES, SUBLANES, LANES)
    sub = lax.broadcasted_iota(jnp.int32, tiles.shape, 1)
    out = None
    for start in range(0, SUBLANES, 2 * h):
        row = jnp.broadcast_to(tiles[:, start + h - 1:start + h, :], tiles.shape)
        out = row if out is None else jnp.where(sub >= start, row, out)
    return out.reshape(c, LANES)


def _gla_kernel(qfig_ref, lbl_ref, nw_ref, s0_ref, tri_ref, lvl_ref, buf_ref,
                o_ref, s_out_ref, st_ref, a_ref, *, layer, heads, rows, seqs):
    del buf_ref
    width = heads * HG_D
    q_ref, f_ref, i_ref, g_ref = (qfig_ref.at[:, base:base + width] for base in (COL_Q, COL_F, COL_I, COL_G))
    c = pl.program_id(1)
    last = pl.num_programs(1) - 1
    seg = CHUNK // seqs
    n_levels = int(math.log2(min(seg, rows)))

    @pl.when(c == 0)
    def _():
        for s in range(seqs):
            for h in range(heads):
                st_ref[s * heads + h] = s0_ref[s, h].T

    logits = lbl_ref[...]
    ex = jnp.exp(logits - jnp.max(logits, axis=0, keepdims=True))
    sm = ex / jnp.sum(ex, axis=0, keepdims=True)
    lb_all = jnp.zeros((1, sm.shape[1]), F32)
    for l in range(1, layer + 1):
        lb_all = lb_all + sm[l:l + 1]

    tri = tri_ref[...]
    lvl = lvl_ref[...]
    valid = lax.broadcasted_iota(jnp.int32, (CHUNK, LANES), 0) < rows

    head_cols = [slice(h * HG_D, (h + 1) * HG_D) for h in range(heads)]
    qs, ks, vs, decays, cums, o_inters = [], [], [], [], [], []
    for h, cols in enumerate(head_cols):
        lb = lb_all[:, cols]
        q = _silu(_pad_rows(q_ref[:, cols], CHUNK))
        f = lb + (1.0 - lb) * jax.nn.sigmoid(_pad_rows(f_ref[:, cols], CHUNK))
        decay = f if rows == CHUNK else jnp.where(valid, f, 1.0)
        qs.append(q)
        ks.append(1.0 - f)
        decays.append(decay)
        cums.append(_cumsum_rows(tri, jnp.log2(decay)))

    stream_of_lane = lax.broadcasted_iota(jnp.int32, (HG_D, CHUNK), 1) // seg
    for h, cols in enumerate(head_cols):
        q, k, cum = qs[h], ks[h], cums[h]
        v32 = _pad_rows(i_ref[:, cols], CHUNK)
        vs.append(v32.astype(BF16))
        q_dec = (q * jnp.exp2(cum)).astype(BF16)
        v_t = v32.T
        if seqs == 1:
            st = st_ref[h]
            o_inters.append(_nt(q_dec, st.astype(BF16)))
            c_last = cum[CHUNK - 1:CHUNK, :]
            k_dec = (k * jnp.exp2(c_last - cum)).astype(BF16)
            st_ref[h] = st * jnp.exp2(c_last) + _dot(v_t.astype(BF16), k_dec)
        else:
            ends = cum.reshape(seqs, seg, HG_D)[:, seg - 1:seg, :]
            c_last = jnp.broadcast_to(ends, (seqs, seg, HG_D)).reshape(CHUNK, HG_D)
            k_dec = (k * jnp.exp2(c_last - cum)).astype(BF16)
            parts = []
            for s in range(seqs):
                st = st_ref[s * heads + h]
                parts.append(_nt(q_dec[s * seg:(s + 1) * seg], st.astype(BF16)))
                v_s = jnp.where(stream_of_lane == s, v_t, 0.0).astype(BF16)
                st_ref[s * heads + h] = st * jnp.exp2(ends[s]) + _dot(v_s, k_dec)
            o_inters.append(jnp.concatenate(parts, axis=0))

    kbs = [k.astype(BF16) for k in ks]
    for h in range(heads):
        a_ref[h] = jnp.where(lvl == N_LEVELS, _nt(qs[h].astype(BF16), kbs[h]), 0.0)
    for h in range(heads):
        pltpu.store(a_ref.at[h], _nt((qs[h] * decays[h]).astype(BF16), kbs[h]), mask=lvl == 0)
    for lev in range(1, n_levels):
        half = 1 << lev
        for h in range(heads):
            q, k, cum = qs[h], ks[h], cums[h]
            if half < SUBLANES:
                z = jnp.exp2(-jnp.abs(cum - _level_boundary(cum, half)))
                pltpu.store(a_ref.at[h], _nt((q * z).astype(BF16), (k * z).astype(BF16)), mask=lvl == lev)
            else:
                q_parts, k_parts, uppers = [], [], []
                for start in range(0, CHUNK, 2 * half):
                    lo, up = slice(start, start + half), slice(start + half, start + 2 * half)
                    mid = cum[start + half - 1:start + half, :]
                    q_parts.append(q[up] * jnp.exp2(cum[up] - mid))
                    k_parts += [k[lo] * jnp.exp2(mid - cum[lo]), k[up]]
                    uppers.append(up)
                p = _nt(jnp.concatenate(q_parts, axis=0).astype(BF16),
                        jnp.concatenate(k_parts, axis=0).astype(BF16))
                for n, up in enumerate(uppers):
                    pltpu.store(a_ref.at[h, up, :], p[n * half:(n + 1) * half, :], mask=lvl[up, :] == lev)

    for h, cols in enumerate(head_cols):
        o = (_dot(a_ref[h].astype(BF16), vs[h]) + o_inters[h])[:rows]
        o = o * lax.rsqrt(jnp.mean(o * o, axis=-1, keepdims=True) + RMS_EPS) * nw_ref[layer:layer + 1, cols]
        o_ref[:, cols] = (o * _silu(g_ref[:, cols])).astype(o_ref.dtype)

    @pl.when(c == last)
    def _():
        for s in range(seqs):
            for h in range(heads):
                s_out_ref[s, h] = st_ref[s * heads + h].T


def _gla(proj, buf, lb_logits, norm_w, s0, consts, *, layer, batch, seq, row0):
    seqs = CHUNK // seq if seq < CHUNK and CHUNK % seq == 0 and batch % (CHUNK // seq) == 0 else 1
    rows = CHUNK if seqs > 1 else min(seq, CHUNK)
    nc = max(seq // CHUNK, 1)
    blk0 = row0 // rows
    w = HG_HEADS * HG_D
    assert (COL_Q, COL_F, COL_I, COL_G) == (0, w, 2 * w, 3 * w)
    tri = consts["tri"] if seqs == 1 else consts["tri_streams"][seqs]
    fix = lambda b, c: (0, 0)
    state = lambda b, c: (b, 0, 0, 0)
    chunk_rows = lambda b, c: (blk0 + b * nc + c, 0)
    return pl.pallas_call(
        functools.partial(_gla_kernel, layer=layer, heads=HG_HEADS, rows=rows, seqs=seqs),
        grid=(batch // seqs, nc),
        in_specs=[pl.BlockSpec((rows, 4 * w), chunk_rows),
                  pl.BlockSpec((DEPTH, w), fix),
                  pl.BlockSpec((DEPTH, w), fix),
                  pl.BlockSpec((seqs, HG_HEADS, HG_D, HG_D), state),
                  pl.BlockSpec((CHUNK, CHUNK), fix),
                  pl.BlockSpec((CHUNK, CHUNK), fix),
                  pl.BlockSpec(memory_space=pl.ANY)],
        out_specs=[pl.BlockSpec((rows, w), chunk_rows),
                   pl.BlockSpec((seqs, HG_HEADS, HG_D, HG_D), state)],
        out_shape=[jax.ShapeDtypeStruct(buf.shape, buf.dtype),
                   jax.ShapeDtypeStruct((batch, HG_HEADS, HG_D, HG_D), F32)],
        scratch_shapes=[pltpu.VMEM((seqs * HG_HEADS, HG_D, HG_D), F32),
                        pltpu.VMEM((HG_HEADS, CHUNK, CHUNK), F32)],
        input_output_aliases={6: 0},
        compiler_params=_params("arbitrary", "arbitrary"),
        name="hgrn2",
    )(proj, lb_logits, norm_w, s0, tri, consts["level"], buf)


def _causal_conv4(raw, ext_ref, w, bias):
    n = raw.shape[0]
    ext_ref[SUBLANES:SUBLANES + n, :] = raw
    out = (bias + w[3:4] * raw + w[2:3] * ext_ref[7:7 + n, :] + w[1:2] * ext_ref[6:6 + n, :]
           + w[0:1] * ext_ref[5:5 + n, :])
    ext_ref[0:SUBLANES, :] = raw[n - SUBLANES:]
    return out


def _ssd_kernel(zxbc_ref, x_ref, wdt_ref, dtb_ref, cw_ref, cb_ref, st_ref, alog_ref, dskip_ref,
                nw_ref, h0_ref, tri_ref, buf_ref, y_ref, h_out_ref, st_out_ref, ht_ref, px_ref, pbc_ref,
                wdtb_ref, *, layer, rows):
    del buf_ref
    z_ref = zxbc_ref.at[:, 0:COL_XS - COL_Z]
    xs_ref = zxbc_ref.at[:, COL_XS - COL_Z:COL_BC - COL_Z]
    bc_ref = zxbc_ref.at[:, COL_BC - COL_Z:COL_BC - COL_Z + SSM_BC]
    c = pl.program_id(1)
    last = pl.num_programs(1) - 1
    tail = SSM_CONV - 1

    @pl.when(c == 0)
    def _():
        wdtb_ref[...] = wdt_ref[...].T.astype(BF16)
        ht_ref[...] = h0_ref[0].T
        buf = jnp.concatenate([jnp.zeros((SUBLANES - tail, SSM_XBC), F32), st_ref[0]], axis=0)
        px_ref[0:SUBLANES, :] = buf[:, :SSM_INNER]
        pbc_ref[0:SUBLANES, :] = buf[:, SSM_INNER:]

    raw_x, raw_bc = xs_ref[...], bc_ref[...]
    st_out_ref[0, :, :SSM_INNER] = raw_x[rows - tail:]
    st_out_ref[0, :, SSM_INNER:] = raw_bc[rows - tail:]
    cw = cw_ref[...]
    cb = cb_ref[layer:layer + 1, :]
    xs = _pad_rows(_silu(_causal_conv4(raw_x, px_ref, cw[:, :SSM_INNER], cb[:, :SSM_INNER])), CHUNK)
    bc = _pad_rows(_silu(_causal_conv4(raw_bc, pbc_ref, cw[:, SSM_INNER:], cb[:, SSM_INNER:])), CHUNK)

    dt_raw = _dot(x_ref[...], wdtb_ref[...]) + dtb_ref[layer:layer + 1, :]
    dt = _pad_rows(jnp.maximum(dt_raw, 0.0) + jnp.log1p(jnp.exp(-jnp.abs(dt_raw))), CHUNK)
    if rows < CHUNK:
        dt = jnp.where(lax.broadcasted_iota(jnp.int32, dt.shape, 0) < rows, dt, 0.0)
    a = dt * (-LOG2_E * jnp.exp(alog_ref[layer:layer + 1, :]))
    cum = _cumsum_rows(tri_ref[...], a)
    c_last = cum[CHUNK - 1:CHUNK, :]
    cum_t = cum.T
    dt_t = dt.T
    wgt_t = (jnp.exp2(c_last - cum) * dt).T
    dec = jnp.exp2(c_last)
    dskip = dskip_ref[layer:layer + 1, :]

    causal = (lax.broadcasted_iota(jnp.int32, (CHUNK, CHUNK), 0)
              >= lax.broadcasted_iota(jnp.int32, (CHUNK, CHUNK), 1))
    low = lax.broadcasted_iota(jnp.int32, (CHUNK, LANES), 1) < SSM_P
    z = z_ref[...]
    for g in range(SSM_GROUPS):
        bm = bc[:, g * SSM_N:(g + 1) * SSM_N]
        cm = bc[:, (SSM_GROUPS + g) * SSM_N:(SSM_GROUPS + g + 1) * SSM_N]
        cb_ts = _nt(cm.astype(BF16), bm.astype(BF16))
        bm_t = bm.T
        y_parts = []
        for pair in range(SSM_HPG // 2):
            h0 = g * SSM_HPG + 2 * pair
            cols = slice((h0 // 2) * LANES, (h0 // 2 + 1) * LANES)
            xp = xs[:, cols]
            ht = ht_ref[:, cols]
            lhs, rhs, upd_l, upd_r = [], [], [], []
            for h, mine in ((h0, low), (h0 + 1, jnp.logical_not(low))):
                col_b = jnp.broadcast_to(cum[:, h:h + 1], (CHUNK, CHUNK))
                lmat = jnp.exp2(jnp.where(causal, col_b - cum_t[h:h + 1, :], NEG_BIG))
                lhs.append((cb_ts * lmat * dt_t[h:h + 1, :]).astype(BF16))
                lhs.append((cm * jnp.exp2(col_b)).astype(BF16))
                xh = jnp.where(mine, xp, 0.0).astype(BF16)
                rhs.append(xh)
                rhs.append(jnp.where(mine, ht, 0.0).astype(BF16))
                upd_l.append((bm_t * wgt_t[h:h + 1, :]).astype(BF16))
                upd_r.append(xh)
            y_parts.append(_dot(jnp.concatenate(lhs, axis=1), jnp.concatenate(rhs, axis=0)))
            ht_ref[:, cols] = ht * _lane_pair(dec, h0) + _dot(jnp.concatenate(upd_l, axis=1),
                                                             jnp.concatenate(upd_r, axis=0))
            y_parts[-1] = y_parts[-1] + _lane_pair(dskip, h0) * xp
        gcols = slice(g * SSM_GW, (g + 1) * SSM_GW)
        y = jnp.concatenate(y_parts, axis=1)[:rows] * _silu(z[:, gcols])
        y = y * lax.rsqrt(jnp.mean(y * y, axis=-1, keepdims=True) + RMS_EPS) * nw_ref[layer:layer + 1, gcols]
        y_ref[:, gcols] = y.astype(y_ref.dtype)

    @pl.when(c == last)
    def _():
        h_out_ref[0] = ht_ref[...].T


def _ssd(proj, xb, buf, p, h0, conv0, consts, *, layer, batch, seq, row0):
    rows = min(seq, CHUNK)
    nc = seq // rows
    blk0 = row0 // rows
    n = SSM_N
    tail = SSM_CONV - 1
    h0 = h0.reshape(batch, SSM_INNER, n)
    row = lambda base, width: (lambda b, c: (blk0 + b * nc + c, base // width))
    par = lambda b, c: (0, 0)
    st = lambda b, c: (b, 0, 0)
    assert rows % SUBLANES == 0 and COL_Z % LANES == 0
    zxbc_w = COL_BC + SSM_BC - COL_Z
    zxbc = pl.BlockSpec((pl.Element(rows), pl.Element(zxbc_w)),
                        lambda b, c: (SUBLANES * ((rows // SUBLANES) * (blk0 + b * nc + c)), COL_Z))
    y, h_out, conv_out = pl.pallas_call(
        functools.partial(_ssd_kernel, layer=layer, rows=rows),
        grid=(batch, nc),
        in_specs=[zxbc,
                  pl.BlockSpec((rows, D_MODEL), row(0, D_MODEL)),
                  pl.BlockSpec((None, LANES, D_MODEL), lambda b, c: (layer, COL_DT // LANES, 0)),
                  pl.BlockSpec((DEPTH, LANES), par),
                  pl.BlockSpec((None, SSM_CONV, SSM_XBC), lambda b, c: (layer, 0, 0)),
                  pl.BlockSpec((DEPTH, SSM_XBC), par),
                  pl.BlockSpec((1, tail, SSM_XBC), st),
                  pl.BlockSpec((DEPTH, LANES), par),
                  pl.BlockSpec((DEPTH, LANES), par),
                  pl.BlockSpec((DEPTH, SSM_INNER), par),
                  pl.BlockSpec((1, SSM_INNER, n), st),
                  pl.BlockSpec((CHUNK, CHUNK), par),
                  pl.BlockSpec(memory_space=pl.ANY)],
        out_specs=[pl.BlockSpec((rows, SSM_INNER), row(0, SSM_INNER)),
                   pl.BlockSpec((1, SSM_INNER, n), st),
                   pl.BlockSpec((1, tail, SSM_XBC), st)],
        out_shape=[jax.ShapeDtypeStruct(buf.shape, buf.dtype),
                   jax.ShapeDtypeStruct((batch, SSM_INNER, n), F32),
                   jax.ShapeDtypeStruct((batch, tail, SSM_XBC), F32)],
        scratch_shapes=[pltpu.VMEM((n, SSM_INNER), F32),
                        pltpu.VMEM((SUBLANES + rows, SSM_INNER), F32),
                        pltpu.VMEM((SUBLANES + rows, SSM_BC), F32),
                        pltpu.VMEM((D_MODEL, LANES), BF16)],
        input_output_aliases={12: 0},
        compiler_params=_params("arbitrary", "arbitrary"),
        name="ssd",
    )(proj, xb, p["w_in_t"], p["dt_bias"], p["ssm_conv_w"], p["ssm_conv_b"], conv0, p["a_log"],
      p["d_skip"], p["ssm_norm_w"], h0, consts["tri"], buf)
    return y, h_out.reshape(batch, SSM_HEADS, SSM_P, n), conv_out


def _merge_kernel(oa_ref, ob_ref, wa_ref, wb_ref, ga_ref, gb_ref, o_ref, wab_ref, wbb_ref):
    @pl.when(pl.program_id(1) == 0)
    def _():
        wab_ref[...] = wa_ref[...].astype(BF16)
        wbb_ref[...] = wb_ref[...].astype(BF16)

    o_ref[...] = (jax.nn.sigmoid(ga_ref[...]) * _dot(oa_ref[...], wab_ref[...])
                  + jax.nn.sigmoid(gb_ref[...]) * _dot(ob_ref[...], wbb_ref[...])).astype(o_ref.dtype)


def _merge(o_a, o_b, w_a, w_b, proj, layer, tn=1024):
    m, k = o_a.shape
    wspec = pl.BlockSpec((None, k, tn), lambda j, i: (layer, 0, j), pipeline_mode=pl.Buffered(1))
    return pl.pallas_call(
        _merge_kernel,
        grid=(D_MODEL // tn, m // TM),
        in_specs=[pl.BlockSpec((TM, k), lambda j, i: (i, 0)),
                  pl.BlockSpec((TM, k), lambda j, i: (i, 0)),
                  wspec, wspec,
                  pl.BlockSpec((TM, tn), lambda j, i: (i, COL_GA // tn + j)),
                  pl.BlockSpec((TM, tn), lambda j, i: (i, COL_GB // tn + j))],
        out_specs=pl.BlockSpec((TM, tn), lambda j, i: (i, j)),
        out_shape=jax.ShapeDtypeStruct((m, D_MODEL), BF16),
        scratch_shapes=[pltpu.VMEM((k, tn), BF16), pltpu.VMEM((k, tn), BF16)],
        compiler_params=_params("arbitrary", "arbitrary"),
        name="merge",
    )(o_a, o_b, w_a, w_b, proj, proj)


def _out_ln_kernel(mg_ref, w_ref, x_ref, g_ref, b_ref, o_ref, ob_ref, wb_ref, *, layer):
    @pl.when(pl.program_id(0) == 0)
    def _():
        wb_ref[...] = w_ref[...].astype(BF16)

    h = ALPHA * x_ref[...] + _dot(mg_ref[...], wb_ref[...])
    y = _layer_norm(h, g_ref[layer:layer + 1, :], b_ref[layer:layer + 1, :])
    o_ref[...] = y
    ob_ref[...] = y.astype(ob_ref.dtype)


def _out_ln(merged, w_out, x, g, b, layer):
    m, k = merged.shape
    row = lambda i: (i, 0)
    fix = lambda i: (0, 0)
    return pl.pallas_call(
        functools.partial(_out_ln_kernel, layer=layer),
        grid=(m // TM_LN,),
        in_specs=[pl.BlockSpec((TM_LN, k), row),
                  pl.BlockSpec((None, k, D_MODEL), lambda i: (layer, 0, 0), pipeline_mode=pl.Buffered(1)),
                  pl.BlockSpec((TM_LN, D_MODEL), row),
                  pl.BlockSpec((DEPTH, D_MODEL), fix),
                  pl.BlockSpec((DEPTH, D_MODEL), fix)],
        out_specs=[pl.BlockSpec((TM_LN, D_MODEL), row), pl.BlockSpec((TM_LN, D_MODEL), row)],
        out_shape=[jax.ShapeDtypeStruct((m, D_MODEL), F32), jax.ShapeDtypeStruct((m, D_MODEL), BF16)],
        scratch_shapes=[pltpu.VMEM((k, D_MODEL), BF16)],
        compiler_params=_params("arbitrary"),
        name="out_ln",
    )(merged, w_out, x, g, b)


def _gelu(x):
    return 0.5 * x * (1.0 + lax.erf(x * (1.0 / math.sqrt(2.0))))


def _ffn_up_kernel(x_ref, wa_ref, wv_ref, cw_ref, cb_ref, st_ref, h_ref, st_s_ref, st_p_ref,
                   wab_ref, wvb_ref, prev_ref, meta_ref, *, layer, slot_rows, meta_slot, tiles_per_seq):
    i = pl.program_id(1)
    tn = wab_ref.shape[1]
    tail = FFN_CONV - 1

    @pl.when(i == 0)
    def _():
        wab_ref[...] = wa_ref[...].astype(BF16)
        wvb_ref[...] = wv_ref[...].astype(BF16)

    x = x_ref[...]
    a = _dot(x, wab_ref[...])
    w = cw_ref[...]
    bias = cb_ref[layer:layer + 1, :]

    def gated(conv):
        act = _gelu(conv).reshape(TM, tn)
        return (act * _dot(x, wvb_ref[...])).astype(h_ref.dtype)

    @pl.when(i == 0)
    def _():
        slots = TM // slot_rows
        n_s = st_ref.shape[0]
        a3 = a.reshape(slots, slot_rows, tn)
        prev = jnp.concatenate(
            [jnp.concatenate([jnp.zeros((n_s, SUBLANES - tail, tn), F32), st_ref[...]], axis=1),
             jnp.zeros((slots - n_s, SUBLANES, tn), F32)], axis=0)
        ext = jnp.concatenate([prev, a3], axis=1)
        n = slot_rows
        conv = bias + w[2:3] * a3 + w[1:2] * ext[:, 7:7 + n] + w[0:1] * ext[:, 6:6 + n]
        st_s_ref[...] = a3[:n_s, n - tail:]
        meta_ref[...] = a3[meta_slot, N_META - tail:N_META]
        h_ref[...] = gated(conv)

    @pl.when(i > 0)
    def _():
        @pl.when((i - 1) % tiles_per_seq == 0)
        def _():
            prev_ref[...] = jnp.concatenate([jnp.zeros((SUBLANES - tail, tn), F32), meta_ref[...]], axis=0)

        ext = jnp.concatenate([prev_ref[...], a], axis=0)
        conv = bias + w[2:3] * a + w[1:2] * ext[7:7 + TM] + w[0:1] * ext[6:6 + TM]
        prev_ref[...] = a[TM - SUBLANES:]
        st_p_ref[0] = a[TM - tail:]
        h_ref[...] = gated(conv)


def _ffn_up(xb, w_up, conv_w, conv_b, st_sample, layer, *, batch, seq, slot_rows, meta_slot, tn=512):
    m, k = xb.shape
    nj = D_FF // tn
    tail = FFN_CONV - 1
    tps = seq // TM
    n_s = st_sample.shape[1]
    return pl.pallas_call(
        functools.partial(_ffn_up_kernel, layer=layer, slot_rows=slot_rows, meta_slot=meta_slot,
                          tiles_per_seq=tps),
        grid=(nj, m // TM),
        in_specs=[pl.BlockSpec((TM, k), lambda j, i: (i, 0)),
                  pl.BlockSpec((None, k, tn), lambda j, i: (layer, 0, j)),
                  pl.BlockSpec((None, k, tn), lambda j, i: (layer, 0, nj + j)),
                  pl.BlockSpec((None, FFN_CONV, tn), lambda j, i: (layer, 0, j)),
                  pl.BlockSpec((DEPTH, tn), lambda j, i: (0, j)),
                  pl.BlockSpec((None, n_s, tail, tn), lambda j, i: (layer, 0, 0, j))],
        out_specs=[pl.BlockSpec((TM, tn), lambda j, i: (i, j)),
                   pl.BlockSpec((n_s, tail, tn), lambda j, i: (0, 0, j)),
                   pl.BlockSpec((1, tail, tn), lambda j, i: (jnp.maximum(i - 1, 0) // tps, 0, j))],
        out_shape=[jax.ShapeDtypeStruct((m, D_FF), BF16),
                   jax.ShapeDtypeStruct((n_s, tail, D_FF), F32),
                   jax.ShapeDtypeStruct((batch, tail, D_FF), F32)],
        scratch_shapes=[pltpu.VMEM((k, tn), BF16), pltpu.VMEM((k, tn), BF16),
                        pltpu.VMEM((SUBLANES, tn), F32), pltpu.VMEM((tail, tn), F32)],
        compiler_params=_params("arbitrary", "arbitrary"),
        name="ffn_up",
    )(xb, w_up, w_up, conv_w, conv_b, st_sample)


def _ffn_down_kernel(h_ref, w_ref, x_ref, g_ref, b_ref, o_ref, ob_ref, *, layer):
    y = _layer_norm(ALPHA * x_ref[...] + _dot(h_ref[...], w_ref[...]),
                    g_ref[layer:layer + 1, :], b_ref[layer:layer + 1, :])
    o_ref[...] = y
    ob_ref[...] = y.astype(ob_ref.dtype)


def _ffn_down_final_kernel(h_ref, w_ref, x_ref, g_ref, b_ref, ys_ref, yp_ref, *, layer, prompt_tile0):
    i = pl.program_id(0)
    y = _layer_norm(ALPHA * x_ref[...] + _dot(h_ref[...], w_ref[...]),
                    g_ref[layer:layer + 1, :], b_ref[layer:layer + 1, :])

    @pl.when(i == 0)
    def _():
        ys_ref[...] = y

    @pl.when(i >= prompt_tile0)
    def _():
        yp_ref[...] = y


def _ffn_down(h, w_down_b, x, g, b, layer, *, final, n_sample, row_prompt):
    m, k = h.shape
    row = lambda i: (i, 0)
    fix = lambda i: (0, 0)
    in_specs = [pl.BlockSpec((TM_LN, k), row),
                pl.BlockSpec((None, k, D_MODEL), lambda i: (layer, 0, 0), pipeline_mode=pl.Buffered(1)),
                pl.BlockSpec((TM_LN, D_MODEL), row),
                pl.BlockSpec((DEPTH, D_MODEL), fix),
                pl.BlockSpec((DEPTH, D_MODEL), fix)]
    if not final:
        return pl.pallas_call(
            functools.partial(_ffn_down_kernel, layer=layer),
            grid=(m // TM_LN,),
            in_specs=in_specs,
            out_specs=[pl.BlockSpec((TM_LN, D_MODEL), row), pl.BlockSpec((TM_LN, D_MODEL), row)],
            out_shape=[jax.ShapeDtypeStruct((m, D_MODEL), F32), jax.ShapeDtypeStruct((m, D_MODEL), BF16)],
            compiler_params=_params("arbitrary"),
            name="ffn_down",
        )(h, w_down_b, x, g, b)
    assert n_sample == TM_LN and row_prompt % TM_LN == 0
    t0 = row_prompt // TM_LN
    return pl.pallas_call(
        functools.partial(_ffn_down_final_kernel, layer=layer, prompt_tile0=t0),
        grid=(m // TM_LN,),
        in_specs=in_specs,
        out_specs=[pl.BlockSpec((TM_LN, D_MODEL), fix),
                   pl.BlockSpec((TM_LN, D_MODEL), lambda i: (jnp.maximum(i - t0, 0), 0))],
        out_shape=[jax.ShapeDtypeStruct((n_sample, D_MODEL), F32),
                   jax.ShapeDtypeStruct((m - row_prompt, D_MODEL), F32)],
        compiler_params=_params("arbitrary"),
        name="ffn_down_final",
    )(h, w_down_b, x, g, b)


def _constants():
    t = jnp.arange(CHUNK)[:, None]
    s = jnp.arange(CHUNK)[None, :]
    level = jnp.full((CHUNK, CHUNK), -1, jnp.int32)
    for lev in range(N_LEVELS):
        level = jnp.where((t > s) & (((t ^ s) >> lev) == 1), lev, level)
    level = jnp.where(t == s, N_LEVELS, level)
    tri_streams = {n: ((t >= s) & (t // (CHUNK // n) == s // (CHUNK // n))).astype(BF16) for n in (2, 4, 8)}
    return dict(tri=(t >= s).astype(BF16), level=level, tri_streams=tri_streams)


def _head_lanes(v):
    return jnp.pad(v.astype(F32), ((0, 0), (0, LANES - SSM_HEADS)))


def kernel(x_prompt, x_sample, state_hgrn, state_ssm, state_ssm_conv, state_ffn_conv, meta_tokens, w_in,
           hgrn_lb_logits, hgrn_norm_w, w_proj_a, ssm_conv_w, ssm_conv_b, ssm_dt_bias, ssm_a_log, ssm_d,
           ssm_norm_w, w_proj_b, w_out, ln1_g, ln1_b, ffn_w_up, ffn_conv_w, ffn_conv_b, ffn_w_down, ln2_g, ln2_b):
    batch, seq, _ = x_prompt.shape
    dec_batch, dec_seq, _ = x_sample.shape
    n_sample = dec_batch * dec_seq
    row_meta = n_sample
    row_prompt = TM
    assert seq % TM == 0 and dec_seq >= N_META and row_meta + dec_seq <= TM and row_meta % dec_seq == 0
    m = row_prompt + batch * seq

    x, xb = _token_slab(x_sample.reshape(n_sample, D_MODEL).astype(F32), meta_tokens.astype(F32),
                        x_prompt.reshape(batch * seq, D_MODEL).astype(F32))
    consts = _constants()
    lb_logits = hgrn_lb_logits.astype(F32)
    w_in_t = jnp.swapaxes(w_in, 1, 2)
    ssm_p = dict(ssm_conv_w=ssm_conv_w, ssm_conv_b=ssm_conv_b, a_log=_head_lanes(ssm_a_log),
                 d_skip=_head_lanes(ssm_d), ssm_norm_w=ssm_norm_w, w_in_t=w_in_t,
                 dt_bias=_head_lanes(ssm_dt_bias))
    w_down_b = _cast_bf16(ffn_w_down)
    groups = (dict(batch=1, seq=N_META, row0=row_meta), dict(batch=batch, seq=seq, row0=row_prompt),
              dict(batch=dec_batch, seq=dec_seq, row0=0))

    o_a = jnp.zeros((m, HG_HEADS * HG_D), BF16)
    o_b = jnp.zeros((m, SSM_INNER), BF16)

    hg_p, ssm_pp, conv_p, ffn_p, hg_s, ssm_s, conv_s, ffn_s = ([] for _ in range(8))
    y_sample = y_prompt = None
    for l in range(DEPTH):
        proj = _in_proj(xb, w_in_t, l)

        o_a, hg_m = _gla(proj, o_a, lb_logits, hgrn_norm_w, jnp.zeros((1, HG_HEADS, HG_D, HG_D), F32), consts,
                         layer=l, **groups[0])
        o_b, ssm_m, conv_m = _ssd(proj, xb, o_b, ssm_p, jnp.zeros((1, SSM_HEADS, SSM_P, SSM_N), F32),
                                  jnp.zeros((1, SSM_CONV - 1, SSM_XBC), F32), consts, layer=l, **groups[0])
        seed = lambda s: jnp.broadcast_to(s, (batch,) + s.shape[1:])
        o_a, s_hg = _gla(proj, o_a, lb_logits, hgrn_norm_w, seed(hg_m), consts, layer=l, **groups[1])
        o_b, s_ssm, s_conv = _ssd(proj, xb, o_b, ssm_p, seed(ssm_m), seed(conv_m), consts, layer=l, **groups[1])
        hg_p.append(s_hg)
        ssm_pp.append(s_ssm)
        conv_p.append(s_conv)
        o_a, s_hg = _gla(proj, o_a, lb_logits, hgrn_norm_w, state_hgrn[l], consts, layer=l, **groups[2])
        o_b, s_ssm, s_conv = _ssd(proj, xb, o_b, ssm_p, state_ssm[l], state_ssm_conv[l], consts, layer=l,
                                  **groups[2])
        hg_s.append(s_hg)
        ssm_s.append(s_ssm)
        conv_s.append(s_conv)

        merged = _merge(o_a, o_b, w_proj_a, w_proj_b, proj, l)
        x, xb = _out_ln(merged, w_out, x, ln1_g, ln1_b, l)
        hid, f_s, f_p = _ffn_up(xb, ffn_w_up, ffn_conv_w, ffn_conv_b, state_ffn_conv, l, batch=batch, seq=seq,
                                slot_rows=dec_seq, meta_slot=row_meta // dec_seq)
        ffn_s.append(f_s)
        ffn_p.append(f_p)
        if l < DEPTH - 1:
            x, xb = _ffn_down(hid, w_down_b, x, ln2_g, ln2_b, l, final=False, n_sample=n_sample,
                              row_prompt=row_prompt)
        else:
            y_sample, y_prompt = _ffn_down(hid, w_down_b, x, ln2_g, ln2_b, l, final=True, n_sample=n_sample,
                                           row_prompt=row_prompt)
    return (y_prompt.reshape(batch, seq, D_MODEL), y_sample.reshape(dec_batch, dec_seq, D_MODEL),
            jnp.stack(hg_p), jnp.stack(ssm_pp), jnp.stack(conv_p), jnp.stack(ffn_p),
            jnp.stack(hg_s), jnp.stack(ssm_s), jnp.stack(conv_s), jnp.stack(ffn_s))
```

```python
import functools
import math

import jax
import jax.numpy as jnp
from jax import lax
from jax.experimental import pallas as pl
from jax.experimental.pallas import tpu as pltpu

F32 = jnp.float32
BF16 = jnp.bfloat16

D_MODEL = 2048
DEPTH = 2
N_META = 16
HG_HEADS = 16
HG_D = 128
SSM_HEADS = 32
SSM_P = 64
SSM_GROUPS = 4
SSM_HPG = SSM_HEADS // SSM_GROUPS
SSM_N = 128
SSM_INNER = SSM_HEADS * SSM_P
SSM_GW = SSM_HPG * SSM_P
SSM_BC = 2 * SSM_GROUPS * SSM_N
SSM_CONV = 4
SSM_XBC = SSM_INNER + SSM_BC
D_FF = 5632
FFN_CONV = 3
ALPHA = (2.0 * DEPTH) ** 0.25
LN_EPS = 1e-5
RMS_EPS = 1e-6

LANES = 128
SUBLANES = 8
CHUNK = 128
N_LEVELS = int(math.log2(CHUNK))
TM = 512
TM_LN = 256
TM_PROJ_MAX = 1152
VMEM_LIMIT = 48 * 1024 * 1024
NEG_BIG = -1e30
LOG2_E = 1.4426950408889634

COL_Q, COL_F, COL_I, COL_G, COL_Z = 0, 2048, 4096, 6144, 8192
COL_XS, COL_BC = 10240, 12288
N_MAIN = 13312
COL_DT = 13312
COL_GATES = COL_DT + SSM_HEADS
COL_GA, COL_GB = N_MAIN, N_MAIN + D_MODEL


def _params(*semantics):
    return pltpu.CompilerParams(dimension_semantics=semantics, vmem_limit_bytes=VMEM_LIMIT)


def _nt(a, b):
    return lax.dot_general(a, b, (((1,), (1,)), ((), ())), preferred_element_type=F32)


def _dot(a, b):
    return jnp.dot(a, b, preferred_element_type=F32)


def _cumsum_rows(tri_b, g):
    n = g.shape[1]
    hi = g.astype(BF16)
    lo = (g - hi.astype(F32)).astype(BF16)
    s = _dot(tri_b, jnp.concatenate([hi, lo], axis=1))
    return s[:, :n] + s[:, n:]


def _silu(x):
    return x * jax.nn.sigmoid(x)


def _layer_norm(h, g, b):
    mu = jnp.mean(h, axis=-1, keepdims=True)
    d = h - mu
    var = jnp.mean(d * d, axis=-1, keepdims=True)
    return d * lax.rsqrt(var + LN_EPS) * g + b


def _pad_rows(x, rows):
    if x.shape[0] == rows:
        return x
    return jnp.concatenate([x, jnp.zeros((rows - x.shape[0],) + x.shape[1:], x.dtype)], axis=0)


def _lane_pair(row, h0):
    lane = lax.broadcasted_iota(jnp.int32, (1, LANES), 1)
    return jnp.where(lane < SSM_P, jnp.broadcast_to(row[:, h0:h0 + 1], (1, LANES)),
                     jnp.broadcast_to(row[:, h0 + 1:h0 + 2], (1, LANES)))


def _cast_kernel(x_ref, o_ref):
    o_ref[...] = x_ref[...].astype(o_ref.dtype)


def _cast_bf16(w, rows=512):
    lead, r, c = w.shape
    return pl.pallas_call(
        _cast_kernel,
        grid=(lead, r // rows),
        in_specs=[pl.BlockSpec((None, rows, c), lambda l, i: (l, i, 0))],
        out_specs=pl.BlockSpec((None, rows, c), lambda l, i: (l, i, 0)),
        out_shape=jax.ShapeDtypeStruct(w.shape, BF16),
        compiler_params=_params("arbitrary", "arbitrary"),
        name="cast_bf16",
    )(w)


def _slab_kernel(xs_ref, meta_ref, xp_ref, o_ref, ob_ref):
    i = pl.program_id(0)

    @pl.when(i == 0)
    def _():
        n_s, n_m = xs_ref.shape[0], meta_ref.shape[0]
        head = jnp.concatenate([xs_ref[...], meta_ref[...], jnp.zeros((TM - n_s - n_m, D_MODEL), F32)], axis=0)
        o_ref[...] = head
        ob_ref[...] = head.astype(ob_ref.dtype)

    @pl.when(i > 0)
    def _():
        o_ref[...] = xp_ref[...]
        ob_ref[...] = xp_ref[...].astype(ob_ref.dtype)


def _token_slab(x_sample, meta, x_prompt):
    m = TM + x_prompt.shape[0]
    fix = lambda i: (0, 0)
    return pl.pallas_call(
        _slab_kernel,
        grid=(m // TM,),
        in_specs=[pl.BlockSpec(x_sample.shape, fix), pl.BlockSpec(meta.shape, fix),
                  pl.BlockSpec((TM, D_MODEL), lambda i: (jnp.maximum(i - 1, 0), 0))],
        out_specs=[pl.BlockSpec((TM, D_MODEL), lambda i: (i, 0)), pl.BlockSpec((TM, D_MODEL), lambda i: (i, 0))],
        out_shape=[jax.ShapeDtypeStruct((m, D_MODEL), F32), jax.ShapeDtypeStruct((m, D_MODEL), BF16)],
        compiler_params=_params("arbitrary"),
        name="token_slab",
    )(x_sample, meta, x_prompt)


def _in_proj_kernel(x_ref, w_ref, o_ref, wb_ref):
    @pl.when(pl.program_id(1) == 0)
    def _():
        wb_ref[...] = w_ref[0].T.astype(BF16)

    o_ref[...] = _dot(x_ref[...], wb_ref[...])


def _row_tile(m, cap):
    return max(t for t in range(16, cap + 1, 16) if m % t == 0)


def _in_proj(xb, w_in_t, layer, tn=1024):
    m, k = xb.shape
    tm = _row_tile(m, TM_PROJ_MAX)
    n_main = N_MAIN // tn
    n_tiles = n_main + 2 * D_MODEL // tn
    assert COL_GATES % SUBLANES == 0 and tn % SUBLANES == 0

    def window(j, i):
        row8 = jnp.where(j < n_main, j * (tn // SUBLANES), COL_GATES // SUBLANES + (j - n_main) * (tn // SUBLANES))
        return (layer, SUBLANES * row8, 0)

    return pl.pallas_call(
        _in_proj_kernel,
        grid=(n_tiles, m // tm),
        in_specs=[pl.BlockSpec((tm, k), lambda j, i: (i, 0)),
                  pl.BlockSpec((pl.Element(1), pl.Element(tn), pl.Element(k)), window)],
        out_specs=pl.BlockSpec((tm, tn), lambda j, i: (i, j)),
        out_shape=jax.ShapeDtypeStruct((m, n_tiles * tn), F32),
        scratch_shapes=[pltpu.VMEM((k, tn), BF16)],
        compiler_params=_params("arbitrary", "arbitrary"),
        name="in_proj",
    )(xb, w_in_t)


def _level_boundary(cum, h):
    c = cum.shape[0]
    if h >= SUBLANES:
        blk = cum.reshape(c // (2 * h), 2 * h, LANES)
        return jnp.broadcast_to(blk[:, h - 1:h, :], blk.shape).reshape(c, LANES)
    tiles = cum.reshape(c // SUBLANES, SUBLANES, LANES)
    sub = lax.broadcasted_iota(jnp.int32, tiles.shape, 1)
    out = None
    for start in range(0, SUBLANES, 2 * h):
        row = jnp.broadcast_to(tiles[:, start + h - 1:start + h, :], tiles.shape)
        out = row if out is None else jnp.where(sub >= start, row, out)
    return out.reshape(c, LANES)


def _gla_kernel(q_ref, f_ref, i_ref, g_ref, lbl_ref, nw_ref, s0_ref, tri_ref, lvl_ref, buf_ref,
                o_ref, s_out_ref, st_ref, a_ref, *, layer, heads, rows, seqs):
    del buf_ref
    c = pl.program_id(2)
    last = pl.num_programs(2) - 1
    seg = CHUNK // seqs
    n_levels = int(math.log2(min(seg, rows)))

    @pl.when(c == 0)
    def _():
        for s in range(seqs):
            for h in range(heads):
                st_ref[s * heads + h] = s0_ref[s, h].T

    logits = lbl_ref[...]
    ex = jnp.exp(logits - jnp.max(logits, axis=0, keepdims=True))
    sm = ex / jnp.sum(ex, axis=0, keepdims=True)
    lb_all = jnp.zeros((1, sm.shape[1]), F32)
    for l in range(1, layer + 1):
        lb_all = lb_all + sm[l:l + 1]

    tri = tri_ref[...]
    lvl = lvl_ref[...]
    valid = lax.broadcasted_iota(jnp.int32, (CHUNK, LANES), 0) < rows

    head_cols = [slice(h * HG_D, (h + 1) * HG_D) for h in range(heads)]
    qs, ks, vs, decays, cums, o_inters = [], [], [], [], [], []
    for h, cols in enumerate(head_cols):
        lb = lb_all[:, cols]
        q = _silu(_pad_rows(q_ref[:, cols], CHUNK))
        f = lb + (1.0 - lb) * jax.nn.sigmoid(_pad_rows(f_ref[:, cols], CHUNK))
        decay = f if rows == CHUNK else jnp.where(valid, f, 1.0)
        qs.append(q)
        ks.append(1.0 - f)
        decays.append(decay)
        cums.append(_cumsum_rows(tri, jnp.log2(decay)))

    stream_of_lane = lax.broadcasted_iota(jnp.int32, (HG_D, CHUNK), 1) // seg
    for h, cols in enumerate(head_cols):
        q, k, cum = qs[h], ks[h], cums[h]
        v32 = _pad_rows(i_ref[:, cols], CHUNK)
        vs.append(v32.astype(BF16))
        q_dec = (q * jnp.exp2(cum)).astype(BF16)
        v_t = v32.T
        if seqs == 1:
            st = st_ref[h]
            o_inters.append(_nt(q_dec, st.astype(BF16)))
            c_last = cum[CHUNK - 1:CHUNK, :]
            k_dec = (k * jnp.exp2(c_last - cum)).astype(BF16)
            st_ref[h] = st * jnp.exp2(c_last) + _dot(v_t.astype(BF16), k_dec)
        else:
            ends = cum.reshape(seqs, seg, HG_D)[:, seg - 1:seg, :]
            c_last = jnp.broadcast_to(ends, (seqs, seg, HG_D)).reshape(CHUNK, HG_D)
            k_dec = (k * jnp.exp2(c_last - cum)).astype(BF16)
            parts = []
            for s in range(seqs):
                st = st_ref[s * heads + h]
                parts.append(_nt(q_dec[s * seg:(s + 1) * seg], st.astype(BF16)))
                v_s = jnp.where(stream_of_lane == s, v_t, 0.0).astype(BF16)
                st_ref[s * heads + h] = st * jnp.exp2(ends[s]) + _dot(v_s, k_dec)
            o_inters.append(jnp.concatenate(parts, axis=0))

    kbs = [k.astype(BF16) for k in ks]
    for h in range(heads):
        a_ref[h] = jnp.where(lvl == N_LEVELS, _nt(qs[h].astype(BF16), kbs[h]), 0.0)
    for h in range(heads):
        pltpu.store(a_ref.at[h], _nt((qs[h] * decays[h]).astype(BF16), kbs[h]), mask=lvl == 0)
    for lev in range(1, n_levels):
        half = 1 << lev
        for h in range(heads):
            q, k, cum = qs[h], ks[h], cums[h]
            if half < SUBLANES:
                z = jnp.exp2(-jnp.abs(cum - _level_boundary(cum, half)))
                pltpu.store(a_ref.at[h], _nt((q * z).astype(BF16), (k * z).astype(BF16)), mask=lvl == lev)
            else:
                q_parts, k_parts, uppers = [], [], []
                for start in range(0, CHUNK, 2 * half):
                    lo, up = slice(start, start + half), slice(start + half, start + 2 * half)
                    mid = cum[start + half - 1:start + half, :]
                    q_parts.append(q[up] * jnp.exp2(cum[up] - mid))
                    k_parts += [k[lo] * jnp.exp2(mid - cum[lo]), k[up]]
                    uppers.append(up)
                p = _nt(jnp.concatenate(q_parts, axis=0).astype(BF16),
                        jnp.concatenate(k_parts, axis=0).astype(BF16))
                for n, up in enumerate(uppers):
                    pltpu.store(a_ref.at[h, up, :], p[n * half:(n + 1) * half, :], mask=lvl[up, :] == lev)

    for h, cols in enumerate(head_cols):
        o = (_dot(a_ref[h].astype(BF16), vs[h]) + o_inters[h])[:rows]
        o = o * lax.rsqrt(jnp.mean(o * o, axis=-1, keepdims=True) + RMS_EPS) * nw_ref[layer:layer + 1, cols]
        o_ref[:, cols] = (o * _silu(g_ref[:, cols])).astype(o_ref.dtype)

    @pl.when(c == last)
    def _():
        for s in range(seqs):
            for h in range(heads):
                s_out_ref[s, h] = st_ref[s * heads + h].T


def _gla(proj, buf, lb_logits, norm_w, s0, consts, *, layer, batch, seq, row0, heads_per_step=16):
    seqs = CHUNK // seq if seq < CHUNK and CHUNK % seq == 0 and batch % (CHUNK // seq) == 0 else 1
    rows = CHUNK if seqs > 1 else min(seq, CHUNK)
    nc = max(seq // CHUNK, 1)
    blk0 = row0 // rows
    hp = heads_per_step
    w = hp * HG_D
    tri = consts["tri"] if seqs == 1 else consts["tri_streams"][seqs]

    def col(base):
        return lambda b, hb, c: (blk0 + b * nc + c, base // w + hb)

    par = lambda b, hb, c: (0, hb)
    state = lambda b, hb, c: (b, hb, 0, 0)
    out_rows = lambda b, hb, c: (blk0 + b * nc + c, hb)
    return pl.pallas_call(
        functools.partial(_gla_kernel, layer=layer, heads=hp, rows=rows, seqs=seqs),
        grid=(batch // seqs, HG_HEADS // hp, nc),
        in_specs=[pl.BlockSpec((rows, w), col(COL_Q)),
                  pl.BlockSpec((rows, w), col(COL_F)),
                  pl.BlockSpec((rows, w), col(COL_I)),
                  pl.BlockSpec((rows, w), col(COL_G)),
                  pl.BlockSpec((DEPTH, w), par),
                  pl.BlockSpec((DEPTH, w), par),
                  pl.BlockSpec((seqs, hp, HG_D, HG_D), state),
                  pl.BlockSpec((CHUNK, CHUNK), lambda b, hb, c: (0, 0)),
                  pl.BlockSpec((CHUNK, CHUNK), lambda b, hb, c: (0, 0)),
                  pl.BlockSpec(memory_space=pl.ANY)],
        out_specs=[pl.BlockSpec((rows, w), out_rows),
                   pl.BlockSpec((seqs, hp, HG_D, HG_D), state)],
        out_shape=[jax.ShapeDtypeStruct(buf.shape, buf.dtype),
                   jax.ShapeDtypeStruct((batch, HG_HEADS, HG_D, HG_D), F32)],
        scratch_shapes=[pltpu.VMEM((seqs * hp, HG_D, HG_D), F32), pltpu.VMEM((hp, CHUNK, CHUNK), F32)],
        input_output_aliases={9: 0},
        compiler_params=_params("arbitrary", "arbitrary", "arbitrary"),
        name="hgrn2",
    )(proj, proj, proj, proj, lb_logits, norm_w, s0, tri, consts["level"], buf)


def _causal_conv4(raw, ext_ref, w, bias):
    n = raw.shape[0]
    ext_ref[SUBLANES:SUBLANES + n, :] = raw
    out = (bias + w[3:4] * raw + w[2:3] * ext_ref[7:7 + n, :] + w[1:2] * ext_ref[6:6 + n, :]
           + w[0:1] * ext_ref[5:5 + n, :])
    ext_ref[0:SUBLANES, :] = raw[n - SUBLANES:]
    return out


def _ssd_kernel(xs_ref, bc_ref, z_ref, x_ref, wdt_ref, dtb_ref, cw_ref, cb_ref, st_ref, alog_ref, dskip_ref,
                nw_ref, h0_ref, tri_ref, buf_ref, y_ref, h_out_ref, st_out_ref, ht_ref, px_ref, pbc_ref,
                wdtb_ref, *, layer, rows):
    del buf_ref
    c = pl.program_id(1)
    last = pl.num_programs(1) - 1
    tail = SSM_CONV - 1

    @pl.when(c == 0)
    def _():
        wdtb_ref[...] = wdt_ref[...].T.astype(BF16)
        ht_ref[...] = h0_ref[0].T
        buf = jnp.concatenate([jnp.zeros((SUBLANES - tail, SSM_XBC), F32), st_ref[0]], axis=0)
        px_ref[0:SUBLANES, :] = buf[:, :SSM_INNER]
        pbc_ref[0:SUBLANES, :] = buf[:, SSM_INNER:]

    raw_x, raw_bc = xs_ref[...], bc_ref[...]
    st_out_ref[0, :, :SSM_INNER] = raw_x[rows - tail:]
    st_out_ref[0, :, SSM_INNER:] = raw_bc[rows - tail:]
    cw = cw_ref[...]
    cb = cb_ref[layer:layer + 1, :]
    xs = _pad_rows(_silu(_causal_conv4(raw_x, px_ref, cw[:, :SSM_INNER], cb[:, :SSM_INNER])), CHUNK)
    bc = _pad_rows(_silu(_causal_conv4(raw_bc, pbc_ref, cw[:, SSM_INNER:], cb[:, SSM_INNER:])), CHUNK)

    dt_raw = _dot(x_ref[...], wdtb_ref[...]) + dtb_ref[layer:layer + 1, :]
    dt = _pad_rows(jnp.maximum(dt_raw, 0.0) + jnp.log1p(jnp.exp(-jnp.abs(dt_raw))), CHUNK)
    if rows < CHUNK:
        dt = jnp.where(lax.broadcasted_iota(jnp.int32, dt.shape, 0) < rows, dt, 0.0)
    a = dt * (-LOG2_E * jnp.exp(alog_ref[layer:layer + 1, :]))
    cum = _cumsum_rows(tri_ref[...], a)
    c_last = cum[CHUNK - 1:CHUNK, :]
    cum_t = cum.T
    dt_t = dt.T
    wgt_t = (jnp.exp2(c_last - cum) * dt).T
    dec = jnp.exp2(c_last)
    dskip = dskip_ref[layer:layer + 1, :]

    causal = (lax.broadcasted_iota(jnp.int32, (rows, CHUNK), 0)
              >= lax.broadcasted_iota(jnp.int32, (rows, CHUNK), 1))
    low = lax.broadcasted_iota(jnp.int32, (CHUNK, LANES), 1) < SSM_P
    z = z_ref[...]
    for g in range(SSM_GROUPS):
        bm = bc[:, g * SSM_N:(g + 1) * SSM_N]
        cm = bc[:, (SSM_GROUPS + g) * SSM_N:(SSM_GROUPS + g + 1) * SSM_N]
        cb_ts = _nt(cm[:rows].astype(BF16), bm.astype(BF16))
        bm_t = bm.T
        y_parts = []
        for pair in range(SSM_HPG // 2):
            h0 = g * SSM_HPG + 2 * pair
            cols = slice((h0 // 2) * LANES, (h0 // 2 + 1) * LANES)
            xp = xs[:, cols]
            ht = ht_ref[:, cols]
            lhs, rhs, upd_l, upd_r = [], [], [], []
            for h, mine in ((h0, low), (h0 + 1, jnp.logical_not(low))):
                col_b = jnp.broadcast_to(cum[:rows, h:h + 1], (rows, CHUNK))
                lmat = jnp.exp2(jnp.where(causal, col_b - cum_t[h:h + 1, :], NEG_BIG))
                lhs.append((cb_ts * lmat * dt_t[h:h + 1, :]).astype(BF16))
                lhs.append((cm[:rows] * jnp.exp2(col_b)).astype(BF16))
                xh = jnp.where(mine, xp, 0.0).astype(BF16)
                rhs.append(xh)
                rhs.append(jnp.where(mine, ht, 0.0).astype(BF16))
                upd_l.append((bm_t * wgt_t[h:h + 1, :]).astype(BF16))
                upd_r.append(xh)
            y_parts.append(_dot(jnp.concatenate(lhs, axis=1), jnp.concatenate(rhs, axis=0)))
            ht_ref[:, cols] = ht * _lane_pair(dec, h0) + _dot(jnp.concatenate(upd_l, axis=1),
                                                             jnp.concatenate(upd_r, axis=0))
            y_parts[-1] = y_parts[-1] + _lane_pair(dskip, h0) * xp[:rows]
        gcols = slice(g * SSM_GW, (g + 1) * SSM_GW)
        y = jnp.concatenate(y_parts, axis=1) * _silu(z[:, gcols])
        y = y * lax.rsqrt(jnp.mean(y * y, axis=-1, keepdims=True) + RMS_EPS) * nw_ref[layer:layer + 1, gcols]
        y_ref[:, gcols] = y.astype(y_ref.dtype)

    @pl.when(c == last)
    def _():
        h_out_ref[0] = ht_ref[...].T


def _ssd(proj, xb, buf, p, h0, conv0, consts, *, layer, batch, seq, row0):
    rows = min(seq, CHUNK)
    nc = seq // rows
    blk0 = row0 // rows
    n = SSM_N
    tail = SSM_CONV - 1
    h0 = h0.reshape(batch, SSM_INNER, n)
    row = lambda base, width: (lambda b, c: (blk0 + b * nc + c, base // width))
    par = lambda b, c: (0, 0)
    st = lambda b, c: (b, 0, 0)
    y, h_out, conv_out = pl.pallas_call(
        functools.partial(_ssd_kernel, layer=layer, rows=rows),
        grid=(batch, nc),
        in_specs=[pl.BlockSpec((rows, SSM_INNER), row(COL_XS, SSM_INNER)),
                  pl.BlockSpec((rows, SSM_BC), row(COL_BC, SSM_BC)),
                  pl.BlockSpec((rows, SSM_INNER), row(COL_Z, SSM_INNER)),
                  pl.BlockSpec((rows, D_MODEL), row(0, D_MODEL)),
                  pl.BlockSpec((None, LANES, D_MODEL), lambda b, c: (layer, COL_DT // LANES, 0)),
                  pl.BlockSpec((DEPTH, LANES), par),
                  pl.BlockSpec((None, SSM_CONV, SSM_XBC), lambda b, c: (layer, 0, 0)),
                  pl.BlockSpec((DEPTH, SSM_XBC), par),
                  pl.BlockSpec((1, tail, SSM_XBC), st),
                  pl.BlockSpec((DEPTH, LANES), par),
                  pl.BlockSpec((DEPTH, LANES), par),
                  pl.BlockSpec((DEPTH, SSM_INNER), par),
                  pl.BlockSpec((1, SSM_INNER, n), st),
                  pl.BlockSpec((CHUNK, CHUNK), par),
                  pl.BlockSpec(memory_space=pl.ANY)],
        out_specs=[pl.BlockSpec((rows, SSM_INNER), row(0, SSM_INNER)),
                   pl.BlockSpec((1, SSM_INNER, n), st),
                   pl.BlockSpec((1, tail, SSM_XBC), st)],
        out_shape=[jax.ShapeDtypeStruct(buf.shape, buf.dtype),
                   jax.ShapeDtypeStruct((batch, SSM_INNER, n), F32),
                   jax.ShapeDtypeStruct((batch, tail, SSM_XBC), F32)],
        scratch_shapes=[pltpu.VMEM((n, SSM_INNER), F32),
                        pltpu.VMEM((SUBLANES + rows, SSM_INNER), F32),
                        pltpu.VMEM((SUBLANES + rows, SSM_BC), F32),
                        pltpu.VMEM((D_MODEL, LANES), BF16)],
        input_output_aliases={14: 0},
        compiler_params=_params("arbitrary", "arbitrary"),
        name="ssd",
    )(proj, proj, proj, xb, p["w_in_t"], p["dt_bias"], p["ssm_conv_w"], p["ssm_conv_b"], conv0, p["a_log"],
      p["d_skip"], p["ssm_norm_w"], h0, consts["tri"], buf)
    return y, h_out.reshape(batch, SSM_HEADS, SSM_P, n), conv_out


def _merge_kernel(oa_ref, ob_ref, wa_ref, wb_ref, ga_ref, gb_ref, o_ref, wab_ref, wbb_ref):
    @pl.when(pl.program_id(1) == 0)
    def _():
        wab_ref[...] = wa_ref[...].astype(BF16)
        wbb_ref[...] = wb_ref[...].astype(BF16)

    o_ref[...] = (jax.nn.sigmoid(ga_ref[...]) * _dot(oa_ref[...], wab_ref[...])
                  + jax.nn.sigmoid(gb_ref[...]) * _dot(ob_ref[...], wbb_ref[...])).astype(o_ref.dtype)


def _merge(o_a, o_b, w_a, w_b, proj, layer, tn=1024):
    m, k = o_a.shape
    wspec = pl.BlockSpec((None, k, tn), lambda j, i: (layer, 0, j), pipeline_mode=pl.Buffered(1))
    return pl.pallas_call(
        _merge_kernel,
        grid=(D_MODEL // tn, m // TM),
        in_specs=[pl.BlockSpec((TM, k), lambda j, i: (i, 0)),
                  pl.BlockSpec((TM, k), lambda j, i: (i, 0)),
                  wspec, wspec,
                  pl.BlockSpec((TM, tn), lambda j, i: (i, COL_GA // tn + j)),
                  pl.BlockSpec((TM, tn), lambda j, i: (i, COL_GB // tn + j))],
        out_specs=pl.BlockSpec((TM, tn), lambda j, i: (i, j)),
        out_shape=jax.ShapeDtypeStruct((m, D_MODEL), BF16),
        scratch_shapes=[pltpu.VMEM((k, tn), BF16), pltpu.VMEM((k, tn), BF16)],
        compiler_params=_params("arbitrary", "arbitrary"),
        name="merge",
    )(o_a, o_b, w_a, w_b, proj, proj)


def _out_ln_kernel(mg_ref, w_ref, x_ref, g_ref, b_ref, o_ref, ob_ref, wb_ref, *, layer):
    @pl.when(pl.program_id(0) == 0)
    def _():
        wb_ref[...] = w_ref[...].astype(BF16)

    h = ALPHA * x_ref[...] + _dot(mg_ref[...], wb_ref[...])
    y = _layer_norm(h, g_ref[layer:layer + 1, :], b_ref[layer:layer + 1, :])
    o_ref[...] = y
    ob_ref[...] = y.astype(ob_ref.dtype)


def _out_ln(merged, w_out, x, g, b, layer):
    m, k = merged.shape
    row = lambda i: (i, 0)
    fix = lambda i: (0, 0)
    return pl.pallas_call(
        functools.partial(_out_ln_kernel, layer=layer),
        grid=(m // TM_LN,),
        in_specs=[pl.BlockSpec((TM_LN, k), row),
                  pl.BlockSpec((None, k, D_MODEL), lambda i: (layer, 0, 0), pipeline_mode=pl.Buffered(1)),
                  pl.BlockSpec((TM_LN, D_MODEL), row),
                  pl.BlockSpec((DEPTH, D_MODEL), fix),
                  pl.BlockSpec((DEPTH, D_MODEL), fix)],
        out_specs=[pl.BlockSpec((TM_LN, D_MODEL), row), pl.BlockSpec((TM_LN, D_MODEL), row)],
        out_shape=[jax.ShapeDtypeStruct((m, D_MODEL), F32), jax.ShapeDtypeStruct((m, D_MODEL), BF16)],
        scratch_shapes=[pltpu.VMEM((k, D_MODEL), BF16)],
        compiler_params=_params("arbitrary"),
        name="out_ln",
    )(merged, w_out, x, g, b)


def _gelu(x):
    return 0.5 * x * (1.0 + lax.erf(x * (1.0 / math.sqrt(2.0))))


def _ffn_up_kernel(x_ref, wa_ref, wv_ref, cw_ref, cb_ref, st_ref, h_ref, st_s_ref, st_p_ref,
                   wab_ref, wvb_ref, prev_ref, meta_ref, *, layer, slot_rows, meta_slot, tiles_per_seq):
    i = pl.program_id(1)
    tn = wab_ref.shape[1]
    tail = FFN_CONV - 1

    @pl.when(i == 0)
    def _():
        wab_ref[...] = wa_ref[...].astype(BF16)
        wvb_ref[...] = wv_ref[...].astype(BF16)

    x = x_ref[...]
    a = _dot(x, wab_ref[...])
    w = cw_ref[...]
    bias = cb_ref[layer:layer + 1, :]

    def gated(conv):
        act = _gelu(conv).reshape(TM, tn)
        return (act * _dot(x, wvb_ref[...])).astype(h_ref.dtype)

    @pl.when(i == 0)
    def _():
        slots = TM // slot_rows
        n_s = st_ref.shape[0]
        a3 = a.reshape(slots, slot_rows, tn)
        prev = jnp.concatenate(
            [jnp.concatenate([jnp.zeros((n_s, SUBLANES - tail, tn), F32), st_ref[...]], axis=1),
             jnp.zeros((slots - n_s, SUBLANES, tn), F32)], axis=0)
        ext = jnp.concatenate([prev, a3], axis=1)
        n = slot_rows
        conv = bias + w[2:3] * a3 + w[1:2] * ext[:, 7:7 + n] + w[0:1] * ext[:, 6:6 + n]
        st_s_ref[...] = a3[:n_s, n - tail:]
        meta_ref[...] = a3[meta_slot, N_META - tail:N_META]
        h_ref[...] = gated(conv)

    @pl.when(i > 0)
    def _():
        @pl.when((i - 1) % tiles_per_seq == 0)
        def _():
            prev_ref[...] = jnp.concatenate([jnp.zeros((SUBLANES - tail, tn), F32), meta_ref[...]], axis=0)

        ext = jnp.concatenate([prev_ref[...], a], axis=0)
        conv = bias + w[2:3] * a + w[1:2] * ext[7:7 + TM] + w[0:1] * ext[6:6 + TM]
        prev_ref[...] = a[TM - SUBLANES:]
        st_p_ref[0] = a[TM - tail:]
        h_ref[...] = gated(conv)


def _ffn_up(xb, w_up, conv_w, conv_b, st_sample, layer, *, batch, seq, slot_rows, meta_slot, tn=512):
    m, k = xb.shape
    nj = D_FF // tn
    tail = FFN_CONV - 1
    tps = seq // TM
    n_s = st_sample.shape[1]
    return pl.pallas_call(
        functools.partial(_ffn_up_kernel, layer=layer, slot_rows=slot_rows, meta_slot=meta_slot,
                          tiles_per_seq=tps),
        grid=(nj, m // TM),
        in_specs=[pl.BlockSpec((TM, k), lambda j, i: (i, 0)),
                  pl.BlockSpec((None, k, tn), lambda j, i: (layer, 0, j)),
                  pl.BlockSpec((None, k, tn), lambda j, i: (layer, 0, nj + j)),
                  pl.BlockSpec((None, FFN_CONV, tn), lambda j, i: (layer, 0, j)),
                  pl.BlockSpec((DEPTH, tn), lambda j, i: (0, j)),
                  pl.BlockSpec((None, n_s, tail, tn), lambda j, i: (layer, 0, 0, j))],
        out_specs=[pl.BlockSpec((TM, tn), lambda j, i: (i, j)),
                   pl.BlockSpec((n_s, tail, tn), lambda j, i: (0, 0, j)),
                   pl.BlockSpec((1, tail, tn), lambda j, i: (jnp.maximum(i - 1, 0) // tps, 0, j))],
        out_shape=[jax.ShapeDtypeStruct((m, D_FF), BF16),
                   jax.ShapeDtypeStruct((n_s, tail, D_FF), F32),
                   jax.ShapeDtypeStruct((batch, tail, D_FF), F32)],
        scratch_shapes=[pltpu.VMEM((k, tn), BF16), pltpu.VMEM((k, tn), BF16),
                        pltpu.VMEM((SUBLANES, tn), F32), pltpu.VMEM((tail, tn), F32)],
        compiler_params=_params("arbitrary", "arbitrary"),
        name="ffn_up",
    )(xb, w_up, w_up, conv_w, conv_b, st_sample)


def _ffn_down_kernel(h_ref, w_ref, x_ref, g_ref, b_ref, o_ref, ob_ref, *, layer):
    y = _layer_norm(ALPHA * x_ref[...] + _dot(h_ref[...], w_ref[...]),
                    g_ref[layer:layer + 1, :], b_ref[layer:layer + 1, :])
    o_ref[...] = y
    ob_ref[...] = y.astype(ob_ref.dtype)


def _ffn_down_final_kernel(h_ref, w_ref, x_ref, g_ref, b_ref, ys_ref, yp_ref, *, layer, prompt_tile0):
    i = pl.program_id(0)
    y = _layer_norm(ALPHA * x_ref[...] + _dot(h_ref[...], w_ref[...]),
                    g_ref[layer:layer + 1, :], b_ref[layer:layer + 1, :])

    @pl.when(i == 0)
    def _():
        ys_ref[...] = y

    @pl.when(i >= prompt_tile0)
    def _():
        yp_ref[...] = y


def _ffn_down(h, w_down_b, x, g, b, layer, *, final, n_sample, row_prompt):
    m, k = h.shape
    row = lambda i: (i, 0)
    fix = lambda i: (0, 0)
    in_specs = [pl.BlockSpec((TM_LN, k), row),
                pl.BlockSpec((None, k, D_MODEL), lambda i: (layer, 0, 0), pipeline_mode=pl.Buffered(1)),
                pl.BlockSpec((TM_LN, D_MODEL), row),
                pl.BlockSpec((DEPTH, D_MODEL), fix),
                pl.BlockSpec((DEPTH, D_MODEL), fix)]
    if not final:
        return pl.pallas_call(
            functools.partial(_ffn_down_kernel, layer=layer),
            grid=(m // TM_LN,),
            in_specs=in_specs,
            out_specs=[pl.BlockSpec((TM_LN, D_MODEL), row), pl.BlockSpec((TM_LN, D_MODEL), row)],
            out_shape=[jax.ShapeDtypeStruct((m, D_MODEL), F32), jax.ShapeDtypeStruct((m, D_MODEL), BF16)],
            compiler_params=_params("arbitrary"),
            name="ffn_down",
        )(h, w_down_b, x, g, b)
    assert n_sample == TM_LN and row_prompt % TM_LN == 0
    t0 = row_prompt // TM_LN
    return pl.pallas_call(
        functools.partial(_ffn_down_final_kernel, layer=layer, prompt_tile0=t0),
        grid=(m // TM_LN,),
        in_specs=in_specs,
        out_specs=[pl.BlockSpec((TM_LN, D_MODEL), fix),
                   pl.BlockSpec((TM_LN, D_MODEL), lambda i: (jnp.maximum(i - t0, 0), 0))],
        out_shape=[jax.ShapeDtypeStruct((n_sample, D_MODEL), F32),
                   jax.ShapeDtypeStruct((m - row_prompt, D_MODEL), F32)],
        compiler_params=_params("arbitrary"),
        name="ffn_down_final",
    )(h, w_down_b, x, g, b)


def _constants():
    t = jnp.arange(CHUNK)[:, None]
    s = jnp.arange(CHUNK)[None, :]
    level = jnp.full((CHUNK, CHUNK), -1, jnp.int32)
    for lev in range(N_LEVELS):
        level = jnp.where((t > s) & (((t ^ s) >> lev) == 1), lev, level)
    level = jnp.where(t == s, N_LEVELS, level)
    tri_streams = {n: ((t >= s) & (t // (CHUNK // n) == s // (CHUNK // n))).astype(BF16) for n in (2, 4, 8)}
    return dict(tri=(t >= s).astype(BF16), level=level, tri_streams=tri_streams)


def _head_lanes(v):
    return jnp.pad(v.astype(F32), ((0, 0), (0, LANES - SSM_HEADS)))


def kernel(x_prompt, x_sample, state_hgrn, state_ssm, state_ssm_conv, state_ffn_conv, meta_tokens, w_in,
           hgrn_lb_logits, hgrn_norm_w, w_proj_a, ssm_conv_w, ssm_conv_b, ssm_dt_bias, ssm_a_log, ssm_d,
           ssm_norm_w, w_proj_b, w_out, ln1_g, ln1_b, ffn_w_up, ffn_conv_w, ffn_conv_b, ffn_w_down, ln2_g, ln2_b):
    batch, seq, _ = x_prompt.shape
    dec_batch, dec_seq, _ = x_sample.shape
    n_sample = dec_batch * dec_seq
    row_meta = n_sample
    row_prompt = TM
    assert seq % TM == 0 and dec_seq >= N_META and row_meta + dec_seq <= TM and row_meta % dec_seq == 0
    m = row_prompt + batch * seq

    x, xb = _token_slab(x_sample.reshape(n_sample, D_MODEL).astype(F32), meta_tokens.astype(F32),
                        x_prompt.reshape(batch * seq, D_MODEL).astype(F32))
    consts = _constants()
    lb_logits = hgrn_lb_logits.astype(F32)
    w_in_t = jnp.swapaxes(w_in, 1, 2)
    ssm_p = dict(ssm_conv_w=ssm_conv_w, ssm_conv_b=ssm_conv_b, a_log=_head_lanes(ssm_a_log),
                 d_skip=_head_lanes(ssm_d), ssm_norm_w=ssm_norm_w, w_in_t=w_in_t,
                 dt_bias=_head_lanes(ssm_dt_bias))
    w_down_b = _cast_bf16(ffn_w_down)
    groups = (dict(batch=1, seq=N_META, row0=row_meta), dict(batch=batch, seq=seq, row0=row_prompt),
              dict(batch=dec_batch, seq=dec_seq, row0=0))

    o_a = jnp.zeros((m, HG_HEADS * HG_D), BF16)
    o_b = jnp.zeros((m, SSM_INNER), BF16)

    hg_p, ssm_pp, conv_p, ffn_p, hg_s, ssm_s, conv_s, ffn_s = ([] for _ in range(8))
    y_sample = y_prompt = None
    for l in range(DEPTH):
        proj = _in_proj(xb, w_in_t, l)

        o_a, hg_m = _gla(proj, o_a, lb_logits, hgrn_norm_w, jnp.zeros((1, HG_HEADS, HG_D, HG_D), F32), consts,
                         layer=l, **groups[0])
        o_b, ssm_m, conv_m = _ssd(proj, xb, o_b, ssm_p, jnp.zeros((1, SSM_HEADS, SSM_P, SSM_N), F32),
                                  jnp.zeros((1, SSM_CONV - 1, SSM_XBC), F32), consts, layer=l, **groups[0])
        seed = lambda s: jnp.broadcast_to(s, (batch,) + s.shape[1:])
        o_a, s_hg = _gla(proj, o_a, lb_logits, hgrn_norm_w, seed(hg_m), consts, layer=l, **groups[1])
        o_b, s_ssm, s_conv = _ssd(proj, xb, o_b, ssm_p, seed(ssm_m), seed(conv_m), consts, layer=l, **groups[1])
        hg_p.append(s_hg)
        ssm_pp.append(s_ssm)
        conv_p.append(s_conv)
        o_a, s_hg = _gla(proj, o_a, lb_logits, hgrn_norm_w, state_hgrn[l], consts, layer=l, **groups[2])
        o_b, s_ssm, s_conv = _ssd(proj, xb, o_b, ssm_p, state_ssm[l], state_ssm_conv[l], consts, layer=l,
                                  **groups[2])
        hg_s.append(s_hg)
        ssm_s.append(s_ssm)
        conv_s.append(s_conv)

        merged = _merge(o_a, o_b, w_proj_a, w_proj_b, proj, l)
        x, xb = _out_ln(merged, w_out, x, ln1_g, ln1_b, l)
        hid, f_s, f_p = _ffn_up(xb, ffn_w_up, ffn_conv_w, ffn_conv_b, state_ffn_conv, l, batch=batch, seq=seq,
                                slot_rows=dec_seq, meta_slot=row_meta // dec_seq)
        ffn_s.append(f_s)
        ffn_p.append(f_p)
        if l < DEPTH - 1:
            x, xb = _ffn_down(hid, w_down_b, x, ln2_g, ln2_b, l, final=False, n_sample=n_sample,
                              row_prompt=row_prompt)
        else:
            y_sample, y_prompt = _ffn_down(hid, w_down_b, x, ln2_g, ln2_b, l, final=True, n_sample=n_sample,
                                           row_prompt=row_prompt)
    return (y_prompt.reshape(batch, seq, D_MODEL), y_sample.reshape(dec_batch, dec_seq, D_MODEL),
            jnp.stack(hg_p), jnp.stack(ssm_pp), jnp.stack(conv_p), jnp.stack(ffn_p),
            jnp.stack(hg_s), jnp.stack(ssm_s), jnp.stack(conv_s), jnp.stack(ffn_s))
```
